```python
import math
import jax, jax.numpy as jnp
from jax import lax
import numpy as np

D_MODEL = 1024
BATCH = 8
SEQ = 2048
DEPTH = 2
DEC_BATCH = 32
DEC_SEQ = 1
PAST_LEN = 8192
PAGE_SIZE = 128

N_A_LAYERS = DEPTH // 2
N_B_LAYERS = DEPTH - N_A_LAYERS
H_A = 4
DK_A = D_MODEL // 2 // H_A
DV_A = D_MODEL // H_A
QK_A = H_A * DK_A
VW_A = H_A * DV_A
GATE_RANK = 16
GATE_TAU = 16.0
GLA_CHUNK = 32
IN_A = 2 * QK_A + 2 * VW_A + GATE_RANK
H_B = 8
DH_B = D_MODEL // (2 * H_B)
DV_B = 2 * DH_B
KW_B = H_B * 2 * DH_B
VW_B = H_B * DV_B
Q_BLOCK = 128
N_EXPERTS = 16
N_GROUPS = 4
EXPERTS_PER_GROUP = N_EXPERTS // N_GROUPS
TOP_K = 2
D_FF_EXPERT = D_MODEL // 2
DEEPNORM_ALPHA = (2.0 * DEPTH) ** 0.25
DEEPNORM_BETA = (8.0 * DEPTH) ** -0.25
LN_EPS = 1e-5
RMS_EPS = 1e-6
NEG_BIG = -1e30

kernel_name = 'yoco_gla_diffattn_grouped_moe_step'


def layer_norm(x, g, b):
    xf = x.astype(jnp.float32)
    mu = jnp.mean(xf, axis=-1, keepdims=True)
    var = jnp.mean(jnp.square(xf - mu), axis=-1, keepdims=True)
    y = (xf - mu) * lax.rsqrt(var + LN_EPS) * g.astype(jnp.float32) + b.astype(jnp.float32)
    return y.astype(x.dtype)


def rms_norm(x, g):
    xf = x.astype(jnp.float32)
    y = xf * lax.rsqrt(jnp.mean(jnp.square(xf), axis=-1, keepdims=True) + RMS_EPS) * g.astype(jnp.float32)
    return y.astype(x.dtype)


def modulate(x, shift, scale):
    return x * (1.0 + scale[:, None, :]) + shift[:, None, :]


def alibi_slopes():
    return jnp.asarray([2.0 ** (-8.0 * (i + 1) / H_B) for i in range(H_B)], jnp.float32)


def gla_project(xm, w_in, w_gate_up, b_gate):
    n, t, _ = xm.shape
    p = xm @ w_in
    q, k, v, r, gd = jnp.split(p, [QK_A, 2 * QK_A, 2 * QK_A + VW_A, 2 * QK_A + 2 * VW_A], axis=-1)
    g = jax.nn.log_sigmoid(gd @ w_gate_up + b_gate) / GATE_TAU
    heads = lambda a: a.reshape(n, t, H_A, -1).transpose(0, 2, 1, 3)
    return heads(q) * (DK_A ** -0.5), heads(k), heads(v), heads(g), r.reshape(n, t, H_A, DV_A)


def gla_chunked(q, k, v, g, s0):
    b_, h_, t_, _ = q.shape
    nc = t_ // GLA_CHUNK
    to_chunks = lambda a: a.reshape(b_, h_, nc, GLA_CHUNK, a.shape[-1]).transpose(2, 0, 1, 3, 4)
    causal = jnp.tril(jnp.ones((GLA_CHUNK, GLA_CHUNK), bool))[:, :, None]

    def step(s, xs):
        qc, kc, vc, gc = xs
        cum = jnp.cumsum(gc, axis=2)
        inter = jnp.einsum('bhtk,bhkv->bhtv', qc * jnp.exp(cum), s)
        diff = cum[:, :, :, None, :] - cum[:, :, None, :, :]
        decay = jnp.exp(jnp.where(causal, diff, -jnp.inf))
        attn = jnp.einsum('bhtk,bhsk,bhtsk->bhts', qc, kc, decay)
        intra = jnp.einsum('bhts,bhsv->bhtv', attn, vc)
        last = cum[:, :, -1:, :]
        s_new = jnp.exp(last[:, :, 0])[..., None] * s + jnp.einsum('bhsk,bhsv->bhkv', kc * jnp.exp(last - cum), vc)
        return s_new, inter + intra

    s_fin, o = lax.scan(step, s0, (to_chunks(q), to_chunks(k), to_chunks(v), to_chunks(g)))
    o = o.transpose(1, 2, 0, 3, 4).reshape(b_, h_, t_, DV_A)
    return o, s_fin


def gla_recurrent(q, k, v, g, s0):
    def step(s, xs):
        qt, kt, vt, gt = xs
        s = jnp.exp(gt)[..., None] * s + kt[..., None] * vt[..., None, :]
        return s, jnp.einsum('bhk,bhkv->bhv', qt, s)
    mv = lambda a: jnp.moveaxis(a, 2, 0)
    s_fin, o = lax.scan(step, s0, (mv(q), mv(k), mv(v), mv(g)))
    return o.transpose(1, 2, 0, 3), s_fin


def gla_output(o, r, gn, w_o):
    n, _, t, _ = o.shape
    o = rms_norm(o.transpose(0, 2, 1, 3), gn) * jax.nn.silu(r)
    return o.reshape(n, t, VW_A) @ w_o


def diff_lambda(lq1, lk1, lq2, lk2, lam_init):
    f = lambda a: a.astype(jnp.float32)
    return jnp.exp(jnp.sum(f(lq1) * f(lk1))) - jnp.exp(jnp.sum(f(lq2) * f(lk2))) + lam_init


def diff_attn_prompt(q, k, v, lam):
    b_, t_ = q.shape[:2]
    nb = t_ // Q_BLOCK
    slopes = alibi_slopes()
    kf = k.astype(jnp.float32)
    k1, k2 = kf[..., :DH_B], kf[..., DH_B:]
    vf = v.astype(jnp.float32)
    kpos = jnp.arange(t_)
    qb = q.reshape(b_, nb, Q_BLOCK, H_B, 2 * DH_B).transpose(1, 0, 2, 3, 4)

    def block(xs):
        qblk, i = xs
        qf = qblk.astype(jnp.float32)
        qpos = i * Q_BLOCK + jnp.arange(Q_BLOCK)
        dist = (qpos[:, None] - kpos[None, :]).astype(jnp.float32)
        bias = jnp.where(dist >= 0, -slopes[:, None, None] * dist, -jnp.inf)
        p1 = jax.nn.softmax(jnp.einsum('bqhd,bkhd->bhqk', qf[..., :DH_B], k1) + bias, axis=-1)
        p2 = jax.nn.softmax(jnp.einsum('bqhd,bkhd->bhqk', qf[..., DH_B:], k2) + bias, axis=-1)
        return jnp.einsum('bhqk,bkhv->bqhv', p1 - lam * p2, vf)

    o = lax.map(block, (qb, jnp.arange(nb)))
    return o.transpose(1, 0, 2, 3, 4).reshape(b_, t_, H_B, DV_B)


def diff_attn_sample(q, k_new, v_new, cache_k, cache_v, page_table, lam):
    db, s_ = q.shape[:2]
    n_pages = page_table.shape[1]
    slopes = alibi_slopes()
    qf = q.astype(jnp.float32)
    q1, q2 = qf[..., :DH_B], qf[..., DH_B:]
    qpos = PAST_LEN + jnp.arange(s_)

    def online(m, l, a, s, vf):
        m_new = jnp.maximum(m, jnp.max(s, axis=-1))
        corr = jnp.exp(m - m_new)
        p = jnp.exp(s - m_new[..., None])
        return m_new, l * corr + jnp.sum(p, axis=-1), a * corr[..., None] + jnp.einsum('bhqk,bkhv->bhqv', p, vf)

    def update(carry, k, v, bias):
        m1, l1, a1, m2, l2, a2 = carry
        kf, vf = k.astype(jnp.float32), v.astype(jnp.float32)
        s1 = jnp.einsum('bqhd,bkhd->bhqk', q1, kf[..., :DH_B]) + bias
        s2 = jnp.einsum('bqhd,bkhd->bhqk', q2, kf[..., DH_B:]) + bias
        m1, l1, a1 = online(m1, l1, a1, s1, vf)
        m2, l2, a2 = online(m2, l2, a2, s2, vf)
        return (m1, l1, a1, m2, l2, a2)

    def page_step(carry, xs):
        phys, j = xs
        kpos = j * PAGE_SIZE + jnp.arange(PAGE_SIZE)
        dist = (qpos[:, None] - kpos[None, :]).astype(jnp.float32)
        bias = -slopes[:, None, None] * dist
        return update(carry, cache_k[phys], cache_v[phys], bias), None

    m0 = jnp.full((db, H_B, s_), NEG_BIG, jnp.float32)
    l0 = jnp.zeros((db, H_B, s_), jnp.float32)
    a0 = jnp.zeros((db, H_B, s_, DV_B), jnp.float32)
    carry, _ = lax.scan(page_step, (m0, l0, a0, m0, l0, a0), (page_table.T, jnp.arange(n_pages)))
    dist = (qpos[:, None] - qpos[None, :]).astype(jnp.float32)
    bias = jnp.where(dist >= 0, -slopes[:, None, None] * dist, -jnp.inf)
    m1, l1, a1, m2, l2, a2 = update(carry, k_new, v_new, bias)
    o = a1 / l1[..., None] - lam * (a2 / l2[..., None])
    return o.transpose(0, 2, 1, 3)


def diff_output(o, subln, lam_init, w_o):
    n, t = o.shape[:2]
    o = rms_norm(o, subln) * (1.0 - lam_init)
    return o.reshape(n, t, VW_B).astype(w_o.dtype) @ w_o


def shared_kv(x, c, w_ada_kv, b_ada_kv, w_kv):
    n, t, _ = x.shape
    sh, sc = jnp.split(c @ w_ada_kv + b_ada_kv, 2, axis=-1)
    kv = modulate(x, sh, sc) @ w_kv
    return kv[..., :KW_B].reshape(n, t, H_B, 2 * DH_B), kv[..., KW_B:].reshape(n, t, H_B, DV_B)


def moe_ffn(xm, w_router, b_router, w_gate, w_up, w_down):
    shp = xm.shape
    xt = xm.reshape(-1, D_MODEL)
    probs = jax.nn.softmax((xt @ w_router + b_router).astype(jnp.float32), axis=-1)
    pg = probs.reshape(-1, N_GROUPS, EXPERTS_PER_GROUP)
    gscore = jnp.sum(lax.top_k(pg, TOP_K)[0], axis=-1)
    gidx = jnp.argmax(gscore, axis=-1)
    in_group = jnp.take_along_axis(pg, gidx[:, None, None], axis=1)[:, 0]
    vals, idx = lax.top_k(in_group, TOP_K)
    wts = vals / jnp.sum(vals, axis=-1, keepdims=True)
    eidx = gidx[:, None] * EXPERTS_PER_GROUP + idx
    comb = jnp.sum(jax.nn.one_hot(eidx, N_EXPERTS, dtype=jnp.float32) * wts[..., None], axis=1).astype(xt.dtype)
    y = jnp.zeros_like(xt)
    for e in range(N_EXPERTS):
        h = jax.nn.silu(xt @ w_gate[e]) * (xt @ w_up[e])
        y = y + comb[:, e:e + 1] * (h @ w_down[e])
    return y.reshape(shp)


def setup_inputs(seed: int = 0) -> dict:
    key = jax.random.key(seed)
    ks = iter(jax.random.split(key, 48))
    nrm = lambda shape, scale: scale * jax.random.normal(next(ks), shape, jnp.float32)
    n_pages = PAST_LEN // PAGE_SIZE
    n_pool = (DEC_BATCH * n_pages * 5) // 4
    d_s = D_MODEL ** -0.5
    beta = DEEPNORM_BETA
    col_a = jnp.concatenate([jnp.ones((2 * QK_A,)), jnp.full((VW_A,), beta), jnp.ones((VW_A + GATE_RANK,))])
    col_kv = jnp.concatenate([jnp.ones((KW_B,)), jnp.full((VW_B,), beta)])
    perm = jax.random.permutation(next(ks), n_pool)[:DEC_BATCH * n_pages]
    return {
        'x_prompt': nrm((BATCH, SEQ, D_MODEL), 1.0),
        'x_sample': nrm((DEC_BATCH, DEC_SEQ, D_MODEL), 1.0),
        'c_prompt': nrm((BATCH, D_MODEL), 1.0),
        'c_sample': nrm((DEC_BATCH, D_MODEL), 1.0),
        'state_gla': nrm((N_A_LAYERS, DEC_BATCH, H_A, DK_A, DV_A), 0.5),
        'cache_k': nrm((n_pool, PAGE_SIZE, H_B, 2 * DH_B), 1.0),
        'cache_v': nrm((n_pool, PAGE_SIZE, H_B, DV_B), 1.0),
        'page_table': perm.reshape(DEC_BATCH, n_pages).astype(jnp.int32),
        'w_ada': nrm((DEPTH, D_MODEL, 6 * D_MODEL), 0.5 * d_s),
        'b_ada': nrm((DEPTH, 6 * D_MODEL), 0.02),
        'ln_mix_g': 1.0 + nrm((DEPTH, D_MODEL), 0.02),
        'ln_mix_b': nrm((DEPTH, D_MODEL), 0.02),
        'ln_ffn_g': 1.0 + nrm((DEPTH, D_MODEL), 0.02),
        'ln_ffn_b': nrm((DEPTH, D_MODEL), 0.02),
        'w_in_a': nrm((N_A_LAYERS, D_MODEL, IN_A), d_s) * col_a,
        'w_gate_up_a': nrm((N_A_LAYERS, GATE_RANK, QK_A), GATE_RANK ** -0.5),
        'b_gate_a': nrm((N_A_LAYERS, QK_A), 0.5),
        'gn_a': 1.0 + nrm((N_A_LAYERS, DV_A), 0.02),
        'w_o_a': nrm((N_A_LAYERS, VW_A, D_MODEL), beta * VW_A ** -0.5),
        'w_ada_kv': nrm((D_MODEL, 2 * D_MODEL), 0.5 * d_s),
        'b_ada_kv': nrm((2 * D_MODEL,), 0.02),
        'w_kv': nrm((D_MODEL, KW_B + VW_B), d_s) * col_kv,
        'w_q_b': nrm((N_B_LAYERS, D_MODEL, KW_B), d_s),
        'lam_q1': nrm((N_B_LAYERS, DH_B), 0.1),
        'lam_k1': nrm((N_B_LAYERS, DH_B), 0.1),
        'lam_q2': nrm((N_B_LAYERS, DH_B), 0.1),
        'lam_k2': nrm((N_B_LAYERS, DH_B), 0.1),
        'subln_b': 1.0 + nrm((N_B_LAYERS, DV_B), 0.02),
        'w_o_b': nrm((N_B_LAYERS, VW_B, D_MODEL), beta * VW_B ** -0.5),
        'w_router': nrm((D_MODEL, N_EXPERTS), d_s),
        'b_router': nrm((N_EXPERTS,), 0.01),
        'w_exp_gate': nrm((DEPTH, N_EXPERTS, D_MODEL, D_FF_EXPERT), d_s),
        'w_exp_up': nrm((DEPTH, N_EXPERTS, D_MODEL, D_FF_EXPERT), beta * d_s),
        'w_exp_down': nrm((DEPTH, N_EXPERTS, D_FF_EXPERT, D_MODEL), beta * D_FF_EXPERT ** -0.5),
    }


def reference(x_prompt, x_sample, c_prompt, c_sample, state_gla, cache_k, cache_v, page_table,
              w_ada, b_ada, ln_mix_g, ln_mix_b, ln_ffn_g, ln_ffn_b,
              w_in_a, w_gate_up_a, b_gate_a, gn_a, w_o_a,
              w_ada_kv, b_ada_kv, w_kv,
              w_q_b, lam_q1, lam_k1, lam_q2, lam_k2, subln_b, w_o_b,
              w_router, b_router, w_exp_gate, w_exp_up, w_exp_down):

    def run(x, c, prompt):
        n, t, _ = x.shape
        gla_states = []
        k_sh = None
        v_sh = None
        for l in range(DEPTH):
            sh1, sc1, g1, sh2, sc2, g2 = jnp.split(c @ w_ada[l] + b_ada[l], 6, axis=-1)
            xm = modulate(x, sh1, sc1)
            if l < N_A_LAYERS:
                q, k, v, g, r = gla_project(xm, w_in_a[l], w_gate_up_a[l], b_gate_a[l])
                if prompt:
                    s0 = jnp.zeros((n, H_A, DK_A, DV_A), x.dtype)
                    o, s_fin = gla_chunked(q, k, v, g, s0)
                else:
                    o, s_fin = gla_recurrent(q, k, v, g, state_gla[l])
                gla_states.append(s_fin)
                mix = gla_output(o, r, gn_a[l], w_o_a[l])
            else:
                lb = l - N_A_LAYERS
                lam_init = 0.8 - 0.6 * math.exp(-0.3 * l)
                lam = diff_lambda(lam_q1[lb], lam_k1[lb], lam_q2[lb], lam_k2[lb], lam_init)
                q = (xm @ w_q_b[lb]).reshape(n, t, H_B, 2 * DH_B) * (DH_B ** -0.5)
                if prompt:
                    o = diff_attn_prompt(q, k_sh, v_sh, lam)
                else:
                    o = diff_attn_sample(q, k_sh, v_sh, cache_k, cache_v, page_table, lam)
                mix = diff_output(o, subln_b[lb], lam_init, w_o_b[lb]).astype(x.dtype)
            x = layer_norm(DEEPNORM_ALPHA * x + g1[:, None, :] * mix, ln_mix_g[l], ln_mix_b[l])
            xm = modulate(x, sh2, sc2)
            ff = moe_ffn(xm, w_router, b_router, w_exp_gate[l], w_exp_up[l], w_exp_down[l])
            x = layer_norm(DEEPNORM_ALPHA * x + g2[:, None, :] * ff, ln_ffn_g[l], ln_ffn_b[l])
            if l == N_A_LAYERS - 1:
                k_sh, v_sh = shared_kv(x, c, w_ada_kv, b_ada_kv, w_kv)
        return x, jnp.stack(gla_states), k_sh, v_sh

    y_prompt, s_prompt, k_prompt, v_prompt = run(x_prompt, c_prompt, True)
    y_sample, s_sample, k_sample, v_sample = run(x_sample, c_sample, False)
    bp, tp = x_prompt.shape[:2]
    k_prompt_pages = k_prompt.reshape(bp, tp // PAGE_SIZE, PAGE_SIZE, H_B, 2 * DH_B)
    v_prompt_pages = v_prompt.reshape(bp, tp // PAGE_SIZE, PAGE_SIZE, H_B, DV_B)
    return (y_prompt, y_sample, s_prompt, s_sample, k_prompt_pages, v_prompt_pages, k_sample, v_sample)
```

```python
import functools
import math

import jax
import jax.numpy as jnp
from jax import lax
from jax.experimental import pallas as pl
from jax.experimental.pallas import tpu as pltpu

f32 = jnp.float32
bf16 = jnp.bfloat16

D = 1024
H_A, DK_A, DV_A = 4, 128, 256
QK_A, VW_A = H_A * DK_A, H_A * DV_A
GATE_RANK = 16
GATE_TAU = 16.0
H_B, DH_B, DV_B = 8, 64, 128
N_EXPERTS, N_GROUPS, GROUP_SIZE = 16, 4, 4
D_FF = 512
N_CLASSES = N_GROUPS * 6
DEPTH = 2
ALPHA = (2.0 * DEPTH) ** 0.25
LN_EPS = 1e-5
RMS_EPS = 1e-6
NEG_BIG = -1e30
PAGE = 128

LANES = 128
VMEM_LIMIT = 48 * 1024 * 1024
TM = 256
TM_E = 256
GLA_C = 64
TQ = 256
PAGES_PER_STEP = 4
DMA_WINDOW = 32
XW = D + LANES


def _bf(a):
    return a.astype(bf16)


def _mm(a, b):
    return jnp.dot(a, b, preferred_element_type=f32)


def _mm_nt(a, b):
    return lax.dot_general(a, b, (((1,), (1,)), ((), ())), preferred_element_type=f32)


def _mm_tn(a, b):
    return lax.dot_general(a, b, (((0,), (0,)), ((), ())), preferred_element_type=f32)


def _params(*sem):
    return pltpu.CompilerParams(dimension_semantics=sem, vmem_limit_bytes=VMEM_LIMIT)


def _layer_norm(h, g, b):
    mu = jnp.mean(h, axis=-1, keepdims=True)
    hc = h - mu
    var = jnp.mean(hc * hc, axis=-1, keepdims=True)
    return hc * lax.rsqrt(var + LN_EPS) * g + b


def _rms_norm(o, g):
    ms = jnp.mean(o * o, axis=-1, keepdims=True)
    return o * lax.rsqrt(ms + RMS_EPS) * g


def _silu(x):
    return x / (1.0 + jnp.exp(-x))


def _row_to_col(row):
    n = row.shape[1]
    eye = lax.broadcasted_iota(jnp.int32, (n, n), 0) == lax.broadcasted_iota(jnp.int32, (n, n), 1)
    return jnp.sum(jnp.where(eye, row, 0.0), axis=1, keepdims=True)


def _ada_kernel(c_ref, w_ref, b_ref, o_ref):
    o_ref[...] = _mm(_bf(c_ref[...]), _bf(w_ref[...])) + b_ref[...]


def _ada(c_all, w3, b3, tn=1024):
    n_l, _, n = w3.shape
    m = c_all.shape[0]
    return pl.pallas_call(
        _ada_kernel,
        grid=(n_l, n // tn),
        in_specs=[pl.BlockSpec((m, D), lambda l, j: (0, 0)),
                  pl.BlockSpec((None, D, tn), lambda l, j: (l, 0, j)),
                  pl.BlockSpec((None, 1, tn), lambda l, j: (l, 0, j))],
        out_specs=pl.BlockSpec((None, m, tn), lambda l, j: (l, 0, j)),
        out_shape=jax.ShapeDtypeStruct((n_l, m, n), f32),
        compiler_params=_params("parallel", "parallel"),
        name="ada",
    )(c_all, w3, b3)


def _mod_spec(mod, tiles_per_block):
    return pl.BlockSpec((None, mod.shape[1], D), lambda i, *_: (i // tiles_per_block, 0, 0))


def _row_spec(tm, width):
    return pl.BlockSpec((tm, width), lambda i, *_: (i, 0))


def _const_spec(shape):
    nd = len(shape)
    return pl.BlockSpec(shape, lambda i, *_: (0,) * nd)


def _proj_a_kernel(x_ref, sh_ref, sc_ref, w_ref, wgd_ref, wgu_ref, bg_ref,
                   q_ref, k_ref, v_ref, r_ref, g_ref):
    xm = _bf(x_ref[...] * (1.0 + sc_ref[...]) + sh_ref[...])
    q_ref[...] = _mm(xm, w_ref[:, 0:QK_A]) * (DK_A ** -0.5)
    k_ref[...] = _mm(xm, w_ref[:, QK_A:2 * QK_A])
    v_ref[...] = _mm(xm, w_ref[:, 2 * QK_A:2 * QK_A + VW_A])
    r_ref[...] = _mm(xm, w_ref[:, 2 * QK_A + VW_A:2 * QK_A + 2 * VW_A])
    gd = _mm(xm, wgd_ref[...])
    z = _mm(_bf(gd), wgu_ref[...]) + bg_ref[...]
    g_ref[...] = (jnp.minimum(z, 0.0) - jnp.log1p(jnp.exp(-jnp.abs(z)))) * (1.0 / GATE_TAU)


def _proj_a(x, sh, sc, tiles_per_block, tm, w_main, w_gd, w_gu, b_g):
    n = x.shape[0]
    outs = [jax.ShapeDtypeStruct((n, QK_A), f32), jax.ShapeDtypeStruct((n, QK_A), f32),
            jax.ShapeDtypeStruct((n, VW_A), f32), jax.ShapeDtypeStruct((n, VW_A), f32),
            jax.ShapeDtypeStruct((n, QK_A), f32)]
    return pl.pallas_call(
        _proj_a_kernel,
        grid=(n // tm,),
        in_specs=[_row_spec(tm, D), _mod_spec(sh, tiles_per_block), _mod_spec(sc, tiles_per_block),
                  _const_spec(w_main.shape), _const_spec(w_gd.shape), _const_spec(w_gu.shape),
                  _const_spec(b_g.shape)],
        out_specs=[_row_spec(tm, QK_A), _row_spec(tm, QK_A), _row_spec(tm, VW_A),
                   _row_spec(tm, VW_A), _row_spec(tm, QK_A)],
        out_shape=outs,
        compiler_params=_params("parallel"),
        name="proj_a",
    )(x, sh, sc, w_main, w_gd, w_gu, b_g)


def _gla_chunk_kernel(q_ref, k_ref, v_ref, r_ref, g_ref, gn_ref, o_ref, s_ref, st_scr, *, n_chunks):
    c = pl.program_id(1)
    C = q_ref.shape[0]

    @pl.when(c == 0)
    def _():
        st_scr[...] = jnp.zeros_like(st_scr)

    row = lax.broadcasted_iota(jnp.int32, (C, C), 0)
    col = lax.broadcasted_iota(jnp.int32, (C, C), 1)
    tril = col <= row
    tril_b = jnp.where(tril, 1.0, 0.0).astype(bf16)
    g = g_ref[...]
    g1 = _bf(g)
    rem = g - g1.astype(f32)
    g2 = _bf(rem)
    g3 = _bf(rem - g2.astype(f32))
    cum = _mm(tril_b, g1) + _mm(tril_b, g2) + _mm(tril_b, g3)
    mid = C // 2 - 1
    for h in range(H_A):
        ks = slice(h * DK_A, (h + 1) * DK_A)
        vs = slice(h * DV_A, (h + 1) * DV_A)
        cum_h = cum[:, ks]
        m = cum_h[mid:mid + 1, :]
        last = cum_h[C - 1:C, :]
        qh = q_ref[:, ks]
        kh = k_ref[:, ks]
        vb = _bf(v_ref[:, vs])
        q_in = _bf(qh * jnp.exp(cum_h))
        q_mid = _bf(qh * jnp.exp(cum_h - m))
        k_mid = _bf(kh * jnp.exp(m - cum_h))
        k_last = _bf(kh * jnp.exp(last - cum_h))
        st = st_scr[h]
        inter = _mm_nt(q_in, _bf(st))
        attn = jnp.where(tril, _mm_nt(q_mid, k_mid), 0.0)
        o = inter + _mm(_bf(attn), vb)
        st_new = st * jnp.exp(last) + _mm_tn(vb, k_last)
        st_scr[h] = st_new
        rh = r_ref[:, vs]
        o_ref[:, vs] = _bf(_rms_norm(o, gn_ref[...]) * _silu(rh))

        @pl.when(c == n_chunks - 1)
        def _():
            s_ref[h] = st_new.T


def _gla_chunk(q, k, v, r, g, gn, n_seq, t):
    n_chunks = t // GLA_C
    rows = lambda w: pl.BlockSpec((GLA_C, w), lambda b, c: (b * n_chunks + c, 0))
    return pl.pallas_call(
        functools.partial(_gla_chunk_kernel, n_chunks=n_chunks),
        grid=(n_seq, n_chunks),
        in_specs=[rows(QK_A), rows(QK_A), rows(VW_A), rows(VW_A), rows(QK_A),
                  pl.BlockSpec((1, DV_A), lambda b, c: (0, 0))],
        out_specs=[rows(VW_A), pl.BlockSpec((None, H_A, DK_A, DV_A), lambda b, c: (b, 0, 0, 0))],
        out_shape=[jax.ShapeDtypeStruct((n_seq * t, VW_A), bf16),
                   jax.ShapeDtypeStruct((n_seq, H_A, DK_A, DV_A), f32)],
        scratch_shapes=[pltpu.VMEM((H_A, DV_A, DK_A), f32)],
        compiler_params=_params("parallel", "arbitrary"),
        name="gla_chunk",
    )(q, k, v, r, g, gn)


def _gla_step_kernel(q_ref, k_ref, v_ref, r_ref, g_ref, gn_ref, s_ref, o_ref, sn_ref):
    for h in range(H_A):
        ks = slice(h * DK_A, (h + 1) * DK_A)
        vs = slice(h * DV_A, (h + 1) * DV_A)
        q_col = _row_to_col(q_ref[:, ks])
        k_col = _row_to_col(k_ref[:, ks])
        eg_col = _row_to_col(jnp.exp(g_ref[:, ks]))
        s_new = eg_col * s_ref[h] + k_col * v_ref[:, vs]
        sn_ref[h] = s_new
        o = jnp.sum(q_col * s_new, axis=0, keepdims=True)
        rh = r_ref[:, vs]
        o_ref[:, vs] = _rms_norm(o, gn_ref[...]) * _silu(rh)


def _gla_step(q, k, v, r, g, gn, state):
    n = q.shape[0]
    vec = lambda w: pl.BlockSpec((None, 1, w), lambda b: (b, 0, 0))
    st = pl.BlockSpec((None, H_A, DK_A, DV_A), lambda b: (b, 0, 0, 0))
    o, s_new = pl.pallas_call(
        _gla_step_kernel,
        grid=(n,),
        in_specs=[vec(QK_A), vec(QK_A), vec(VW_A), vec(VW_A), vec(QK_A),
                  pl.BlockSpec((1, DV_A), lambda b: (0, 0)), st],
        out_specs=[vec(VW_A), st],
        out_shape=[jax.ShapeDtypeStruct((n, 1, VW_A), f32),
                   jax.ShapeDtypeStruct((n, H_A, DK_A, DV_A), f32)],
        compiler_params=_params("parallel"),
        name="gla_step",
    )(q.reshape(n, 1, QK_A), k.reshape(n, 1, QK_A), v.reshape(n, 1, VW_A), r.reshape(n, 1, VW_A),
      g.reshape(n, 1, QK_A), gn, state)
    return o.reshape(n, VW_A), s_new


def _route(probs):
    p = [probs[e:e + 1, :] for e in range(N_EXPERTS)]
    scores = []
    for gi in range(N_GROUPS):
        a0, a1, a2, a3 = p[4 * gi:4 * gi + 4]
        hi1, lo1 = jnp.maximum(a0, a1), jnp.minimum(a0, a1)
        hi2, lo2 = jnp.maximum(a2, a3), jnp.minimum(a2, a3)
        scores.append(jnp.maximum(hi1, hi2) + jnp.maximum(jnp.minimum(hi1, hi2), jnp.maximum(lo1, lo2)))
    best = scores[0]
    gidx = jnp.zeros(best.shape, jnp.int32)
    for gi in range(1, N_GROUPS):
        upd = scores[gi] > best
        best = jnp.where(upd, scores[gi], best)
        gidx = jnp.where(upd, gi, gidx)
    a = [jnp.where(gidx == 0, p[j], jnp.where(gidx == 1, p[4 + j], jnp.where(gidx == 2, p[8 + j], p[12 + j])))
         for j in range(GROUP_SIZE)]
    v1 = a[0]
    i1 = jnp.zeros(best.shape, jnp.int32)
    for j in range(1, GROUP_SIZE):
        upd = a[j] > v1
        v1 = jnp.where(upd, a[j], v1)
        i1 = jnp.where(upd, j, i1)
    v2 = jnp.full(best.shape, -1.0, f32)
    i2 = jnp.zeros(best.shape, jnp.int32)
    for j in range(GROUP_SIZE):
        upd = jnp.logical_and(i1 != j, a[j] > v2)
        v2 = jnp.where(upd, a[j], v2)
        i2 = jnp.where(upd, j, i2)
    tot = v1 + v2
    w1 = v1 / tot
    w2 = v2 / tot
    first_low = i1 < i2
    lo = jnp.minimum(i1, i2)
    hi = jnp.maximum(i1, i2)
    pair = lo * 3 - ((lo * (lo - 1)) >> 1) + hi - lo - 1
    cls = gidx * 6 + pair
    return cls, jnp.where(first_low, w1, w2), jnp.where(first_low, w2, w1)


def _post_kernel(a_ref, w_ref, x_ref, g1_ref, lng_ref, lnb_ref, sh2_ref, sc2_ref, wr_ref, br_ref,
                 x1_ref, rows_ref, meta_ref, cnt_ref, cnt_scr):
    i = pl.program_id(0)
    tm = x_ref.shape[0]

    @pl.when(i == 0)
    def _():
        cnt_scr[...] = jnp.zeros_like(cnt_scr)

    mix = _mm(_bf(a_ref[...]), w_ref[...])
    x1 = _layer_norm(ALPHA * x_ref[...] + g1_ref[...] * mix, lng_ref[...], lnb_ref[...])
    x1_ref[...] = x1
    xm = x1 * (1.0 + sc2_ref[...]) + sh2_ref[...]
    rows_ref[:, 0:D] = xm

    x_hi = _bf(xm)
    x_lo = _bf(xm - x_hi.astype(f32))
    wr = wr_ref[...]
    l_hi = _mm_nt(wr, x_hi)
    l_lo = _mm_nt(wr[0:N_EXPERTS], x_lo)
    logits = l_hi[0:N_EXPERTS] + l_hi[N_EXPERTS:2 * N_EXPERTS] + l_lo + br_ref[...]
    mx = jnp.max(logits, axis=0, keepdims=True)
    ex = jnp.exp(logits - mx)
    probs = ex / jnp.sum(ex, axis=0, keepdims=True)
    cls, w_lo, w_hi = _route(probs)

    crow = lax.broadcasted_iota(jnp.int32, (32, tm), 0)
    onehot = crow == cls
    oh = jnp.where(onehot, 1.0, 0.0)
    before = lax.broadcasted_iota(jnp.int32, (tm, tm), 0) < lax.broadcasted_iota(jnp.int32, (tm, tm), 1)
    prefix = _mm(_bf(oh), jnp.where(before, 1.0, 0.0).astype(bf16))
    carry = cnt_scr[...]
    rank = jnp.sum(jnp.where(onehot, prefix + carry, 0.0), axis=0, keepdims=True)
    cnt_new = carry + jnp.sum(oh, axis=1, keepdims=True)
    cnt_scr[...] = cnt_new
    cnt_ref[...] = jnp.broadcast_to(cnt_new, cnt_ref.shape)

    mrow = lax.broadcasted_iota(jnp.int32, (8, tm), 0)
    meta_ref[...] = jnp.where(mrow == 0, cls, jnp.where(mrow == 1, rank.astype(jnp.int32), 0))

    lane = lax.broadcasted_iota(jnp.int32, (tm, LANES), 1)
    rows_ref[:, D:XW] = jnp.where(lane == 0, _row_to_col(w_lo), jnp.where(lane == 1, _row_to_col(w_hi), 0.0))


def _post(a, w_o, x, g1, lng, lnb, sh2, sc2, tiles_per_block, tm, wr_t, br_col):
    n = x.shape[0]
    mod = lambda arr: _mod_spec(arr, tiles_per_block)
    return pl.pallas_call(
        _post_kernel,
        grid=(n // tm,),
        in_specs=[_row_spec(tm, D), _const_spec(w_o.shape), _row_spec(tm, D), mod(g1),
                  _const_spec(lng.shape), _const_spec(lnb.shape), mod(sh2), mod(sc2),
                  _const_spec(wr_t.shape), _const_spec(br_col.shape)],
        out_specs=[_row_spec(tm, D), _row_spec(tm, XW),
                   pl.BlockSpec((8, tm), lambda i: (0, i)),
                   pl.BlockSpec((32, LANES), lambda i: (0, 0))],
        out_shape=[jax.ShapeDtypeStruct((n, D), f32), jax.ShapeDtypeStruct((n, XW), f32),
                   jax.ShapeDtypeStruct((8, n), jnp.int32), jax.ShapeDtypeStruct((32, LANES), f32)],
        scratch_shapes=[pltpu.VMEM((32, 1), f32)],
        compiler_params=_params("arbitrary"),
        name="post",
    )(a, w_o, x, g1, lng, lnb, sh2, sc2, wr_t, br_col)


def _permute_kernel(pos_ref, src_ref, *rest, n, scatter):
    dst_ref, sem = rest[-2], rest[-1]

    def row_copy(i):
        p = pos_ref[i]
        if scatter:
            return pltpu.make_async_copy(src_ref.at[pl.ds(i, 1)], dst_ref.at[pl.ds(p, 1)], sem)
        return pltpu.make_async_copy(src_ref.at[pl.ds(p, 1)], dst_ref.at[pl.ds(i, 1)], sem)

    window = min(DMA_WINDOW, n)

    def issue(i, carry):
        row_copy(i).start()

        @pl.when(i >= window)
        def _():
            row_copy(i - window).wait()
        return carry

    def drain(i, carry):
        row_copy(i).wait()
        return carry

    lax.fori_loop(0, n, issue, 0)
    lax.fori_loop(n - window, n, drain, 0)


def _scatter_rows(pos, src, dst):
    n = src.shape[0]
    any_spec = pl.BlockSpec(memory_space=pl.ANY)
    return pl.pallas_call(
        functools.partial(_permute_kernel, n=n, scatter=True),
        grid_spec=pltpu.PrefetchScalarGridSpec(
            num_scalar_prefetch=1, grid=(1,),
            in_specs=[any_spec, any_spec], out_specs=any_spec,
            scratch_shapes=[pltpu.SemaphoreType.DMA(())]),
        out_shape=jax.ShapeDtypeStruct(dst.shape, dst.dtype),
        input_output_aliases={2: 0},
        compiler_params=_params("arbitrary"),
        name="scatter_rows",
    )(pos, src, dst)


def _gather_rows(pos, src):
    n = pos.shape[0]
    any_spec = pl.BlockSpec(memory_space=pl.ANY)
    return pl.pallas_call(
        functools.partial(_permute_kernel, n=n, scatter=False),
        grid_spec=pltpu.PrefetchScalarGridSpec(
            num_scalar_prefetch=1, grid=(1,),
            in_specs=[any_spec], out_specs=any_spec,
            scratch_shapes=[pltpu.SemaphoreType.DMA(())]),
        out_shape=jax.ShapeDtypeStruct((n, src.shape[1]), src.dtype),
        compiler_params=_params("arbitrary"),
        name="gather_rows",
    )(pos, src)


def _experts_kernel(elo_ref, ehi_ref, valid_ref, rows_ref,
                    wg_lo, wu_lo, wd_lo, wg_hi, wu_hi, wd_hi, o_ref):
    t = pl.program_id(0)

    @pl.when(valid_ref[t] == 1)
    def _():
        xb = _bf(rows_ref[:, 0:D])

        def ffn(wg, wu, wd):
            hid = _silu(_mm(xb, wg[...])) * _mm(xb, wu[...])
            return _mm(_bf(hid), wd[...])

        o_ref[...] = rows_ref[:, D:D + 1] * ffn(wg_lo, wu_lo, wd_lo) \
            + rows_ref[:, D + 1:D + 2] * ffn(wg_hi, wu_hi, wd_hi)

    @pl.when(valid_ref[t] == 0)
    def _():
        o_ref[...] = jnp.zeros_like(o_ref)


def _experts(e_lo, e_hi, valid, rows, wg, wu, wd, layer):
    n_tiles = rows.shape[0] // TM_E
    up = lambda sel: pl.BlockSpec((None, None, D, D_FF), lambda t, lo, hi, v: (layer, sel(lo, hi)[t], 0, 0))
    down = lambda sel: pl.BlockSpec((None, None, D_FF, D), lambda t, lo, hi, v: (layer, sel(lo, hi)[t], 0, 0))
    first = lambda lo, hi: lo
    second = lambda lo, hi: hi
    return pl.pallas_call(
        _experts_kernel,
        grid_spec=pltpu.PrefetchScalarGridSpec(
            num_scalar_prefetch=3, grid=(n_tiles,),
            in_specs=[pl.BlockSpec((TM_E, XW), lambda t, lo, hi, v: (t, 0)),
                      up(first), up(first), down(first), up(second), up(second), down(second)],
            out_specs=pl.BlockSpec((TM_E, D), lambda t, lo, hi, v: (t, 0))),
        out_shape=jax.ShapeDtypeStruct((rows.shape[0], D), f32),
        compiler_params=_params("parallel"),
        name="experts",
    )(e_lo, e_hi, valid, rows, wg, wu, wd, wg, wu, wd)


def _moe(rows_p, meta_p, cnt_p, rows_s, meta_s, cnt_s, wg, wu, wd, layer):
    n_p, n_s = rows_p.shape[0], rows_s.shape[0]
    n_tiles = -(-(n_p + n_s) // TM_E) + N_CLASSES
    cp = cnt_p[:N_CLASSES, 0].astype(jnp.int32)
    cs = cnt_s[:N_CLASSES, 0].astype(jnp.int32)
    tiles_c = (cp + cs + TM_E - 1) // TM_E
    tile_end = jnp.cumsum(tiles_c)
    row_off = (tile_end - tiles_c) * TM_E
    pos_p = row_off[meta_p[0]] + meta_p[1]
    pos_s = row_off[meta_s[0]] + cp[meta_s[0]] + meta_s[1]
    t_idx = jnp.arange(n_tiles, dtype=jnp.int32)
    valid = (t_idx < tile_end[-1]).astype(jnp.int32)
    tile_cls = jnp.minimum(jnp.sum((tile_end[None, :] <= t_idx[:, None]).astype(jnp.int32), axis=1), N_CLASSES - 1)
    pairs = [(a, b) for a in range(GROUP_SIZE) for b in range(a + 1, GROUP_SIZE)]
    lo_tab = jnp.asarray([gi * GROUP_SIZE + a for gi in range(N_GROUPS) for a, _ in pairs], jnp.int32)
    hi_tab = jnp.asarray([gi * GROUP_SIZE + b for gi in range(N_GROUPS) for _, b in pairs], jnp.int32)
    e_lo, e_hi = lo_tab[tile_cls], hi_tab[tile_cls]

    buf = jnp.zeros((n_tiles * TM_E, XW), f32)
    buf = _scatter_rows(pos_p, rows_p, buf)
    buf = _scatter_rows(pos_s, rows_s, buf)
    ff_sorted = _experts(e_lo, e_hi, valid, buf, wg, wu, wd, layer)
    return _gather_rows(pos_p, ff_sorted), _gather_rows(pos_s, ff_sorted)


def _mid_kernel(x1_ref, ff_ref, g2_ref, lng_ref, lnb_ref, shkv_ref, sckv_ref, shq_ref, scq_ref,
                wkv_ref, wq_ref, x2_ref, k_ref, v_ref, kb_ref, vb_ref, q_ref):
    x2 = _layer_norm(ALPHA * x1_ref[...] + g2_ref[...] * ff_ref[...], lng_ref[...], lnb_ref[...])
    x2_ref[...] = x2
    xkv = _bf(x2 * (1.0 + sckv_ref[...]) + shkv_ref[...])
    kk = _mm(xkv, wkv_ref[:, 0:D])
    vv = _mm(xkv, wkv_ref[:, D:2 * D])
    for h in range(H_B):
        hs = slice(h * DV_B, (h + 1) * DV_B)
        k_ref[:, h, :] = kk[:, hs]
        v_ref[:, h, :] = vv[:, hs]
    kb_ref[...] = _bf(kk)
    vb_ref[...] = _bf(vv)
    xq = _bf(x2 * (1.0 + scq_ref[...]) + shq_ref[...])
    q_ref[...] = _bf(_mm(xq, wq_ref[...]) * (DH_B ** -0.5))


def _mid(x1, ff, g2, lng, lnb, shkv, sckv, shq, scq, tiles_per_block, tm, w_kv, w_q):
    n = x1.shape[0]
    mod = lambda arr: _mod_spec(arr, tiles_per_block)
    sds = lambda dt: jax.ShapeDtypeStruct((n, D), dt)
    kv_sds = jax.ShapeDtypeStruct((n, H_B, DV_B), f32)
    heads = pl.BlockSpec((tm, H_B, DV_B), lambda i: (i, 0, 0))
    return pl.pallas_call(
        _mid_kernel,
        grid=(n // tm,),
        in_specs=[_row_spec(tm, D), _row_spec(tm, D), mod(g2), _const_spec(lng.shape), _const_spec(lnb.shape),
                  mod(shkv), mod(sckv), mod(shq), mod(scq), _const_spec(w_kv.shape), _const_spec(w_q.shape)],
        out_specs=[_row_spec(tm, D), heads, heads, _row_spec(tm, D), _row_spec(tm, D), _row_spec(tm, D)],
        out_shape=[sds(f32), kv_sds, kv_sds, sds(bf16), sds(bf16), sds(bf16)],
        compiler_params=_params("parallel"),
        name="mid",
    )(x1, ff, g2, lng, lnb, shkv, sckv, shq, scq, w_kv, w_q)


def _lambda(lq1_ref, lk1_ref, lq2_ref, lk2_ref, lam_init):
    s1 = jnp.sum(lq1_ref[...] * lk1_ref[...], axis=1, keepdims=True)
    s2 = jnp.sum(lq2_ref[...] * lk2_ref[...], axis=1, keepdims=True)
    return jnp.exp(s1) - jnp.exp(s2) + lam_init


def _online_update(s, v_b, m_scr, l_scr, acc_scr):
    m_old = m_scr[...]
    m_new = jnp.maximum(m_old, jnp.max(s, axis=1, keepdims=True))
    p = jnp.exp(s - m_new)
    corr = jnp.exp(m_old - m_new)
    l_scr[...] = l_scr[...] * corr + jnp.sum(p, axis=1, keepdims=True)
    acc_scr[...] = acc_scr[...] * corr + _mm(_bf(p), v_b)
    m_scr[...] = m_new


def _dattn_kernel(slope_ref, q_ref, k_ref, v_ref, lq1_ref, lk1_ref, lq2_ref, lk2_ref, subln_ref,
                  o_ref, m_scr, l_scr, acc_scr, *, lam_init):
    t = q_ref.shape[0]
    slope = slope_ref[pl.program_id(1)]
    lam = _lambda(lq1_ref, lk1_ref, lq2_ref, lk2_ref, lam_init)
    first_map = lax.broadcasted_iota(jnp.int32, (1, 2 * DH_B), 1) < DH_B
    krel = lax.broadcasted_iota(jnp.int32, (1, TQ), 1)
    qrel = lax.broadcasted_iota(jnp.int32, (2 * TQ, 1), 0)
    qrel = jnp.where(qrel >= TQ, qrel - TQ, qrel)

    def q_block(i, carry):
        q0 = pl.multiple_of(i * TQ, TQ)
        q = q_ref[pl.ds(q0, TQ), :]
        zero = jnp.zeros_like(q)
        qs = jnp.concatenate([jnp.where(first_map, q, zero), jnp.where(first_map, zero, q)], axis=0)
        m_scr[...] = jnp.full(m_scr.shape, NEG_BIG, f32)
        l_scr[...] = jnp.zeros_like(l_scr)
        acc_scr[...] = jnp.zeros_like(acc_scr)

        def k_block(j, masked):
            k0 = pl.multiple_of(j * TQ, TQ)
            s = _mm_nt(qs, k_ref[pl.ds(k0, TQ), :])
            s = s + slope * (krel + (k0 - q0)).astype(f32)
            if masked:
                s = jnp.where(krel <= qrel, s, NEG_BIG)
            _online_update(s, v_ref[pl.ds(k0, TQ), :], m_scr, l_scr, acc_scr)

        def below_diagonal(j, c):
            k_block(j, False)
            return c

        lax.fori_loop(0, i, below_diagonal, 0)
        k_block(i, True)
        acc = acc_scr[...]
        l = l_scr[...]
        o = acc[0:TQ] / l[0:TQ] - lam * (acc[TQ:2 * TQ] / l[TQ:2 * TQ])
        o_ref[pl.ds(q0, TQ), :] = _bf(_rms_norm(o, subln_ref[...]) * (1.0 - lam_init))
        return carry

    lax.fori_loop(0, t // TQ, q_block, 0)


def _dattn(slopes, q, k, v, lq1, lk1, lq2, lk2, subln, n_seq, t, lam_init):
    head = pl.BlockSpec((t, 2 * DH_B), lambda b, h: (b, h))
    small = lambda w: pl.BlockSpec((1, w), lambda b, h: (0, 0))
    return pl.pallas_call(
        functools.partial(_dattn_kernel, lam_init=lam_init),
        grid=(n_seq, H_B),
        in_specs=[pl.BlockSpec(memory_space=pltpu.SMEM), head, head, head,
                  small(DH_B), small(DH_B), small(DH_B), small(DH_B), small(DV_B)],
        out_specs=head,
        out_shape=jax.ShapeDtypeStruct((n_seq * t, H_B * DV_B), bf16),
        scratch_shapes=[pltpu.VMEM((2 * TQ, 1), f32), pltpu.VMEM((2 * TQ, 1), f32),
                        pltpu.VMEM((2 * TQ, DV_B), f32)],
        compiler_params=_params("parallel", "parallel"),
        name="dattn_prompt",
    )(slopes, q, k, v, lq1, lk1, lq2, lk2, subln)


def _dattn_step_kernel(pt_ref, q_ref, kn_ref, vn_ref, nslope_ref, lq1_ref, lk1_ref, lq2_ref, lk2_ref,
                       subln_ref, *rest, past_len, lam_init):
    n_pg = PAGES_PER_STEP
    k_refs, v_refs = rest[0:n_pg], rest[n_pg:2 * n_pg]
    o_ref, qbd_scr, m_scr, l_scr, acc_scr = rest[2 * n_pg:]
    j = pl.program_id(1)
    rows = 2 * H_B

    @pl.when(j == 0)
    def _():
        lane = lax.broadcasted_iota(jnp.int32, (rows, D), 1)
        rowi = lax.broadcasted_iota(jnp.int32, (rows, D), 0)
        q = jnp.broadcast_to(q_ref[...].astype(f32), (rows, D))
        qbd_scr[...] = _bf(jnp.where((lane >> 6) == rowi, q, 0.0))
        m_scr[...] = jnp.full(m_scr.shape, NEG_BIG, f32)
        l_scr[...] = jnp.zeros_like(l_scr)
        acc_scr[...] = jnp.zeros_like(acc_scr)

    def head_major(ref):
        return jnp.concatenate([_bf(ref[:, h, :]) for h in range(H_B)], axis=1)

    qbd = qbd_scr[...]
    for u in range(n_pg):
        s = _mm_nt(qbd, head_major(k_refs[u]))
        kpos = (j * n_pg + u) * PAGE + lax.broadcasted_iota(jnp.int32, (1, PAGE), 1)
        s = s + nslope_ref[...] * (past_len - kpos).astype(f32)
        _online_update(s, head_major(v_refs[u]), m_scr, l_scr, acc_scr)

    @pl.when(j == pl.num_programs(1) - 1)
    def _():
        kn = jnp.broadcast_to(jnp.concatenate([_bf(kn_ref[h:h + 1, :]) for h in range(H_B)], axis=1), (16, D))
        vn = jnp.broadcast_to(jnp.concatenate([_bf(vn_ref[h:h + 1, :]) for h in range(H_B)], axis=1), (16, D))
        s = _mm_nt(qbd, kn)
        s = jnp.where(lax.broadcasted_iota(jnp.int32, (1, 16), 1) == 0, s, NEG_BIG)
        _online_update(s, vn, m_scr, l_scr, acc_scr)
        lane = lax.broadcasted_iota(jnp.int32, (rows, D), 1)
        rowi = lax.broadcasted_iota(jnp.int32, (rows, D), 0)
        a = jnp.where((lane >> 7) == (rowi >> 1), acc_scr[...] / l_scr[...], 0.0)
        a1 = jnp.sum(jnp.where((rowi & 1) == 0, a, 0.0), axis=0, keepdims=True)
        a2 = jnp.sum(jnp.where((rowi & 1) == 1, a, 0.0), axis=0, keepdims=True)
        o = a1 - _lambda(lq1_ref, lk1_ref, lq2_ref, lk2_ref, lam_init) * a2
        for h in range(H_B):
            hs = slice(h * DV_B, (h + 1) * DV_B)
            o_ref[:, hs] = _rms_norm(o[:, hs], subln_ref[...]) * (1.0 - lam_init)


def _dattn_step(page_table, q, k_new, v_new, cache_k, cache_v, nslope_rows, lq1, lk1, lq2, lk2, subln,
                past_len, lam_init):
    n, n_pages = page_table.shape
    n_pg = PAGES_PER_STEP
    vec = pl.BlockSpec((None, 1, D), lambda b, j, pt: (b, 0, 0))
    new_kv = pl.BlockSpec((None, H_B, DV_B), lambda b, j, pt: (b, 0, 0))
    const = lambda shape: pl.BlockSpec(shape, lambda b, j, pt: (0,) * len(shape))

    def page(u):
        return pl.BlockSpec((None, PAGE, H_B, DV_B), lambda b, j, pt: (pt[b, j * n_pg + u], 0, 0, 0))

    pages = [page(u) for u in range(n_pg)]
    out = pl.pallas_call(
        functools.partial(_dattn_step_kernel, past_len=past_len, lam_init=lam_init),
        grid_spec=pltpu.PrefetchScalarGridSpec(
            num_scalar_prefetch=1, grid=(n, n_pages // n_pg),
            in_specs=[vec, new_kv, new_kv, const((2 * H_B, 1)), const((1, DH_B)), const((1, DH_B)),
                      const((1, DH_B)), const((1, DH_B)), const((1, DV_B))] + pages + pages,
            out_specs=vec,
            scratch_shapes=[pltpu.VMEM((2 * H_B, D), bf16), pltpu.VMEM((2 * H_B, 1), f32),
                            pltpu.VMEM((2 * H_B, 1), f32), pltpu.VMEM((2 * H_B, D), f32)]),
        out_shape=jax.ShapeDtypeStruct((n, 1, D), f32),
        compiler_params=_params("parallel", "arbitrary"),
        name="dattn_step",
    )(page_table, q.reshape(n, 1, D), k_new, v_new, nslope_rows,
      lq1, lk1, lq2, lk2, subln, *([cache_k] * n_pg), *([cache_v] * n_pg))
    return out.reshape(n, D)


def _final_kernel(x_ref, ff_ref, g2_ref, lng_ref, lnb_ref, y_ref):
    y_ref[...] = _layer_norm(ALPHA * x_ref[...] + g2_ref[...] * ff_ref[...], lng_ref[...], lnb_ref[...])


def _final(x, ff, g2, lng, lnb, tiles_per_block, tm):
    n = x.shape[0]
    return pl.pallas_call(
        _final_kernel,
        grid=(n // tm,),
        in_specs=[_row_spec(tm, D), _row_spec(tm, D), _mod_spec(g2, tiles_per_block),
                  _const_spec(lng.shape), _const_spec(lnb.shape)],
        out_specs=_row_spec(tm, D),
        out_shape=jax.ShapeDtypeStruct((n, D), f32),
        compiler_params=_params("parallel"),
        name="final",
    )(x, ff, g2, lng, lnb)


def kernel(x_prompt, x_sample, c_prompt, c_sample, state_gla, cache_k, cache_v, page_table, w_ada, b_ada,
           ln_mix_g, ln_mix_b, ln_ffn_g, ln_ffn_b, w_in_a, w_gate_up_a, b_gate_a, gn_a, w_o_a, w_ada_kv,
           b_ada_kv, w_kv, w_q_b, lam_q1, lam_k1, lam_q2, lam_k2, subln_b, w_o_b, w_router, b_router,
           w_exp_gate, w_exp_up, w_exp_down):
    n_seq, t, _ = x_prompt.shape
    n_smp = x_sample.shape[0]
    n_p = n_seq * t
    past_len = page_table.shape[1] * PAGE
    xp = x_prompt.reshape(n_p, D)
    xs = x_sample.reshape(n_smp, D)
    tpb = t // TM

    c_all = jnp.concatenate([c_prompt, c_sample], axis=0)
    ada = _ada(c_all, w_ada, b_ada.reshape(DEPTH, 1, 6 * D))
    ada_kv = _ada(c_all, w_ada_kv.reshape(1, D, 2 * D), b_ada_kv.reshape(1, 1, 2 * D))[0]

    def mods(table, idx):
        part = table[:, idx * D:(idx + 1) * D]
        return part[:n_seq].reshape(n_seq, 1, D), part[n_seq:].reshape(1, n_smp, D)

    row = lambda a: a.reshape(1, -1)

    w_in = w_in_a[0]
    w_main = _bf(w_in[:, :2 * QK_A + 2 * VW_A])
    w_gd = _bf(jnp.pad(w_in[:, 2 * QK_A + 2 * VW_A:], ((0, 0), (0, LANES - GATE_RANK))))
    w_gu = _bf(jnp.pad(w_gate_up_a[0], ((0, LANES - GATE_RANK), (0, 0))))
    wr_t = w_router.T
    wr_hi = _bf(wr_t)
    wr_split = jnp.concatenate([wr_hi, _bf(wr_t - wr_hi.astype(f32))], axis=0)
    br_col = b_router.reshape(N_EXPERTS, 1)
    wg, wu, wd = _bf(w_exp_gate), _bf(w_exp_up), _bf(w_exp_down)

    sh1, sc1, g1, sh2, sc2, g2 = [mods(ada[0], i) for i in range(6)]
    b_g = row(b_gate_a[0])
    qp, kp, vp, rp, gp = _proj_a(xp, sh1[0], sc1[0], tpb, TM, w_main, w_gd, w_gu, b_g)
    qs, ks, vs, rs, gs = _proj_a(xs, sh1[1], sc1[1], 1, n_smp, w_main, w_gd, w_gu, b_g)
    gn = row(gn_a[0])
    ap, s_prompt = _gla_chunk(qp, kp, vp, rp, gp, gn, n_seq, t)
    as_, s_sample = _gla_step(qs, ks, vs, rs, gs, gn, state_gla[0])

    w_oa = _bf(w_o_a[0])
    lng, lnb = row(ln_mix_g[0]), row(ln_mix_b[0])
    x1p, rows_p, meta_p, cnt_p = _post(ap, w_oa, xp, g1[0], lng, lnb, sh2[0], sc2[0], tpb, TM, wr_split, br_col)
    x1s, rows_s, meta_s, cnt_s = _post(as_, w_oa, xs, g1[1], lng, lnb, sh2[1], sc2[1], 1, n_smp, wr_split, br_col)
    ffp, ffs = _moe(rows_p, meta_p, cnt_p, rows_s, meta_s, cnt_s, wg, wu, wd, 0)

    shkv, sckv = mods(ada_kv, 0), mods(ada_kv, 1)
    sh1, sc1, g1b, sh2b, sc2b, g2b = [mods(ada[1], i) for i in range(6)]
    lng, lnb = row(ln_ffn_g[0]), row(ln_ffn_b[0])
    w_kvb, w_qb = _bf(w_kv), _bf(w_q_b[0])
    x2p, k_p, v_p, kb_p, vb_p, q_p = _mid(x1p, ffp, g2[0], lng, lnb, shkv[0], sckv[0], sh1[0], sc1[0],
                                          tpb, TM, w_kvb, w_qb)
    x2s, k_s, v_s, _, _, q_s = _mid(x1s, ffs, g2[1], lng, lnb, shkv[1], sckv[1], sh1[1], sc1[1],
                                    1, n_smp, w_kvb, w_qb)

    lam_init = 0.8 - 0.6 * math.exp(-0.3 * 1)
    slopes = jnp.asarray([2.0 ** (-8.0 * (i + 1) / H_B) for i in range(H_B)], f32)
    nslope_rows = -jnp.repeat(slopes, 2).reshape(2 * H_B, 1)
    lam_args = (row(lam_q1[0]), row(lam_k1[0]), row(lam_q2[0]), row(lam_k2[0]), row(subln_b[0]))
    bp = _dattn(slopes, q_p, kb_p, vb_p, *lam_args, n_seq, t, lam_init)
    bs = _dattn_step(page_table, q_s, k_s, v_s, cache_k, cache_v, nslope_rows, *lam_args, past_len, lam_init)

    w_ob = _bf(w_o_b[0])
    lng, lnb = row(ln_mix_g[1]), row(ln_mix_b[1])
    x3p, rows_p, meta_p, cnt_p = _post(bp, w_ob, x2p, g1b[0], lng, lnb, sh2b[0], sc2b[0], tpb, TM, wr_split, br_col)
    x3s, rows_s, meta_s, cnt_s = _post(bs, w_ob, x2s, g1b[1], lng, lnb, sh2b[1], sc2b[1], 1, n_smp, wr_split, br_col)
    ffp, ffs = _moe(rows_p, meta_p, cnt_p, rows_s, meta_s, cnt_s, wg, wu, wd, 1)
    lng, lnb = row(ln_ffn_g[1]), row(ln_ffn_b[1])
    y_p = _final(x3p, ffp, g2b[0], lng, lnb, tpb, TM)
    y_s = _final(x3s, ffs, g2b[1], lng, lnb, 1, n_smp)

    n_pg_p = t // PAGE
    return (y_p.reshape(n_seq, t, D), y_s.reshape(n_smp, 1, D),
            s_prompt[None], s_sample[None],
            k_p.reshape(n_seq, n_pg_p, PAGE, H_B, 2 * DH_B), v_p.reshape(n_seq, n_pg_p, PAGE, H_B, DV_B),
            k_s[:, None], v_s[:, None])
```

```python
import functools
import math

import jax
import jax.numpy as jnp
from jax import lax
from jax.experimental import pallas as pl
from jax.experimental.pallas import tpu as pltpu

f32 = jnp.float32
bf16 = jnp.bfloat16

D = 1024
H_A, DK_A, DV_A = 4, 128, 256
QK_A, VW_A = H_A * DK_A, H_A * DV_A
GATE_RANK = 16
GATE_TAU = 16.0
H_B, DH_B, DV_B = 8, 64, 128
N_EXPERTS, N_GROUPS, GROUP_SIZE = 16, 4, 4
D_FF = 512
N_CLASSES = N_GROUPS * 6
DEPTH = 2
ALPHA = (2.0 * DEPTH) ** 0.25
LN_EPS = 1e-5
RMS_EPS = 1e-6
NEG_BIG = -1e30
PAGE = 128

LANES = 128
VMEM_LIMIT = 48 * 1024 * 1024
TM = 256
TM_E = 256
GLA_C = 64
GLA_SEQS = 4
TQ = 512
TH = TQ // 2
PAGES_PER_STEP = 4
DMA_WINDOW = 32
SUB = 8
assert D == SUB * LANES


def _load_tokens(ref, n, rows_per_token):
    return jnp.concatenate([ref[pl.ds(h, n, stride=rows_per_token), :] for h in range(SUB)], axis=1)


def _store_tokens(ref, val, offset, rows_per_token):
    n = val.shape[0]
    for h in range(SUB):
        ref[pl.ds(offset + h, n, stride=rows_per_token), :] = val[:, h * LANES:(h + 1) * LANES]


def _bf(a):
    return a.astype(bf16)


def _mm(a, b):
    return jnp.dot(a, b, preferred_element_type=f32)


def _mm_nt(a, b):
    return lax.dot_general(a, b, (((1,), (1,)), ((), ())), preferred_element_type=f32)


def _mm_tn(a, b):
    return lax.dot_general(a, b, (((0,), (0,)), ((), ())), preferred_element_type=f32)


def _params(*sem):
    return pltpu.CompilerParams(dimension_semantics=sem, vmem_limit_bytes=VMEM_LIMIT)


def _layer_norm(h, g, b):
    mu = jnp.mean(h, axis=-1, keepdims=True)
    hc = h - mu
    var = jnp.mean(hc * hc, axis=-1, keepdims=True)
    return hc * lax.rsqrt(var + LN_EPS) * g + b


def _rms_norm(o, g):
    ms = jnp.mean(o * o, axis=-1, keepdims=True)
    return o * lax.rsqrt(ms + RMS_EPS) * g


def _silu(x):
    return x / (1.0 + jnp.exp(-x))


def _row_to_col(row):
    n = row.shape[1]
    eye = lax.broadcasted_iota(jnp.int32, (n, n), 0) == lax.broadcasted_iota(jnp.int32, (n, n), 1)
    return jnp.sum(jnp.where(eye, row, 0.0), axis=1, keepdims=True)


def _ada_kernel(c_ref, w_ref, b_ref, o_ref):
    o_ref[...] = _mm(_bf(c_ref[...]), _bf(w_ref[...])) + b_ref[...]


def _ada(c_all, w3, b3, tn=1024):
    n_l, _, n = w3.shape
    m = c_all.shape[0]
    return pl.pallas_call(
        _ada_kernel,
        grid=(n_l, n // tn),
        in_specs=[pl.BlockSpec((m, D), lambda l, j: (0, 0)),
                  pl.BlockSpec((None, D, tn), lambda l, j: (l, 0, j)),
                  pl.BlockSpec((None, 1, tn), lambda l, j: (l, 0, j))],
        out_specs=pl.BlockSpec((None, m, tn), lambda l, j: (l, 0, j)),
        out_shape=jax.ShapeDtypeStruct((n_l, m, n), f32),
        compiler_params=_params("parallel", "parallel"),
        name="ada",
    )(c_all, w3, b3)


def _mod_spec(mod, tiles_per_block):
    return pl.BlockSpec((None, mod.shape[1], D), lambda i, *_: (i // tiles_per_block, 0, 0))


def _row_spec(tm, width):
    return pl.BlockSpec((tm, width), lambda i, *_: (i, 0))


def _const_spec(shape):
    nd = len(shape)
    return pl.BlockSpec(shape, lambda i, *_: (0,) * nd)


def _proj_a_kernel(x_ref, sh_ref, sc_ref, w_ref, wgd_ref, wgu_ref, bg_ref,
                   q_ref, k_ref, v_ref, r_ref, g_ref):
    xm = _bf(x_ref[...] * (1.0 + sc_ref[...]) + sh_ref[...])
    q_ref[...] = _mm(xm, w_ref[:, 0:QK_A]) * (DK_A ** -0.5)
    k_ref[...] = _mm(xm, w_ref[:, QK_A:2 * QK_A])
    v_ref[...] = _mm(xm, w_ref[:, 2 * QK_A:2 * QK_A + VW_A])
    r_ref[...] = _mm(xm, w_ref[:, 2 * QK_A + VW_A:2 * QK_A + 2 * VW_A])
    gd = _mm(xm, wgd_ref[...])
    z = _mm(_bf(gd), wgu_ref[...]) + bg_ref[...]
    g_ref[...] = (jnp.minimum(z, 0.0) - jnp.log1p(jnp.exp(-jnp.abs(z)))) * (1.0 / GATE_TAU)


def _proj_a(x, sh, sc, tiles_per_block, tm, w_main, w_gd, w_gu, b_g):
    n = x.shape[0]
    outs = [jax.ShapeDtypeStruct((n, QK_A), f32), jax.ShapeDtypeStruct((n, QK_A), f32),
            jax.ShapeDtypeStruct((n, VW_A), f32), jax.ShapeDtypeStruct((n, VW_A), f32),
            jax.ShapeDtypeStruct((n, QK_A), f32)]
    return pl.pallas_call(
        _proj_a_kernel,
        grid=(n // tm,),
        in_specs=[_row_spec(tm, D), _mod_spec(sh, tiles_per_block), _mod_spec(sc, tiles_per_block),
                  _const_spec(w_main.shape), _const_spec(w_gd.shape), _const_spec(w_gu.shape),
                  _const_spec(b_g.shape)],
        out_specs=[_row_spec(tm, QK_A), _row_spec(tm, QK_A), _row_spec(tm, VW_A),
                   _row_spec(tm, VW_A), _row_spec(tm, QK_A)],
        out_shape=outs,
        compiler_params=_params("parallel"),
        name="proj_a",
    )(x, sh, sc, w_main, w_gd, w_gu, b_g)


def _gla_chunk_kernel(q_ref, k_ref, v_ref, r_ref, g_ref, gn_ref, o_ref, s_ref, st_scr, *, n_chunks):
    c = pl.program_id(1)
    C = q_ref.shape[1]

    @pl.when(c == 0)
    def _():
        st_scr[...] = jnp.zeros_like(st_scr)

    row = lax.broadcasted_iota(jnp.int32, (C, C), 0)
    col = lax.broadcasted_iota(jnp.int32, (C, C), 1)
    tril = col <= row
    tril_b = jnp.where(tril, 1.0, 0.0).astype(bf16)
    mid = C // 2 - 1
    for b in range(q_ref.shape[0]):
        g = g_ref[b]
        g1 = _bf(g)
        rem = g - g1.astype(f32)
        g2 = _bf(rem)
        g3 = _bf(rem - g2.astype(f32))
        cum = _mm(tril_b, g1) + _mm(tril_b, g2) + _mm(tril_b, g3)
        for h in range(H_A):
            ks = slice(h * DK_A, (h + 1) * DK_A)
            vs = slice(h * DV_A, (h + 1) * DV_A)
            cum_h = cum[:, ks]
            m = cum_h[mid:mid + 1, :]
            last = cum_h[C - 1:C, :]
            qh = q_ref[b, :, ks]
            kh = k_ref[b, :, ks]
            vb = _bf(v_ref[b, :, vs])
            q_in = _bf(qh * jnp.exp(cum_h))
            q_mid = _bf(qh * jnp.exp(cum_h - m))
            k_mid = _bf(kh * jnp.exp(m - cum_h))
            k_last = _bf(kh * jnp.exp(last - cum_h))
            st = st_scr[b, h]
            inter = _mm_nt(q_in, _bf(st))
            attn = jnp.where(tril, _mm_nt(q_mid, k_mid), 0.0)
            o = inter + _mm(_bf(attn), vb)
            st_new = st * jnp.exp(last) + _mm_tn(vb, k_last)
            st_scr[b, h] = st_new
            rh = r_ref[b, :, vs]
            o_ref[b, :, vs] = _bf(_rms_norm(o, gn_ref[...]) * _silu(rh))

    @pl.when(c == n_chunks - 1)
    def _():
        for b in range(q_ref.shape[0]):
            for h in range(H_A):
                s_ref[b, h] = st_scr[b, h].T


def _gla_chunk(q, k, v, r, g, gn, n_seq, t):
    n_chunks = t // GLA_C
    nb = GLA_SEQS
    rows = lambda w: pl.BlockSpec((nb, GLA_C, w), lambda b, c: (b, c, 0))
    seq = lambda a: a.reshape(n_seq, t, a.shape[1])
    o, s_fin = pl.pallas_call(
        functools.partial(_gla_chunk_kernel, n_chunks=n_chunks),
        grid=(n_seq // nb, n_chunks),
        in_specs=[rows(QK_A), rows(QK_A), rows(VW_A), rows(VW_A), rows(QK_A),
                  pl.BlockSpec((1, DV_A), lambda b, c: (0, 0))],
        out_specs=[rows(VW_A), pl.BlockSpec((nb, H_A, DK_A, DV_A), lambda b, c: (b, 0, 0, 0))],
        out_shape=[jax.ShapeDtypeStruct((n_seq, t, VW_A), bf16),
                   jax.ShapeDtypeStruct((n_seq, H_A, DK_A, DV_A), f32)],
        scratch_shapes=[pltpu.VMEM((nb, H_A, DV_A, DK_A), f32)],
        compiler_params=_params("parallel", "arbitrary"),
        name="gla_chunk",
    )(seq(q), seq(k), seq(v), seq(r), seq(g), gn)
    return o.reshape(n_seq * t, VW_A), s_fin


def _gla_step_kernel(q_ref, k_ref, v_ref, r_ref, g_ref, gn_ref, s_ref, o_ref, sn_ref):
    for h in range(H_A):
        ks = slice(h * DK_A, (h + 1) * DK_A)
        vs = slice(h * DV_A, (h + 1) * DV_A)
        q_col = _row_to_col(q_ref[:, ks])
        k_col = _row_to_col(k_ref[:, ks])
        eg_col = _row_to_col(jnp.exp(g_ref[:, ks]))
        s_new = eg_col * s_ref[h] + k_col * v_ref[:, vs]
        sn_ref[h] = s_new
        o = jnp.sum(q_col * s_new, axis=0, keepdims=True)
        rh = r_ref[:, vs]
        o_ref[:, vs] = _rms_norm(o, gn_ref[...]) * _silu(rh)


def _gla_step(q, k, v, r, g, gn, state):
    n = q.shape[0]
    vec = lambda w: pl.BlockSpec((None, 1, w), lambda b: (b, 0, 0))
    st = pl.BlockSpec((None, H_A, DK_A, DV_A), lambda b: (b, 0, 0, 0))
    o, s_new = pl.pallas_call(
        _gla_step_kernel,
        grid=(n,),
        in_specs=[vec(QK_A), vec(QK_A), vec(VW_A), vec(VW_A), vec(QK_A),
                  pl.BlockSpec((1, DV_A), lambda b: (0, 0)), st],
        out_specs=[vec(VW_A), st],
        out_shape=[jax.ShapeDtypeStruct((n, 1, VW_A), f32),
                   jax.ShapeDtypeStruct((n, H_A, DK_A, DV_A), f32)],
        compiler_params=_params("parallel"),
        name="gla_step",
    )(q.reshape(n, 1, QK_A), k.reshape(n, 1, QK_A), v.reshape(n, 1, VW_A), r.reshape(n, 1, VW_A),
      g.reshape(n, 1, QK_A), gn, state)
    return o.reshape(n, VW_A), s_new


def _route(probs):
    p = [probs[e:e + 1, :] for e in range(N_EXPERTS)]
    scores = []
    for gi in range(N_GROUPS):
        a0, a1, a2, a3 = p[4 * gi:4 * gi + 4]
        hi1, lo1 = jnp.maximum(a0, a1), jnp.minimum(a0, a1)
        hi2, lo2 = jnp.maximum(a2, a3), jnp.minimum(a2, a3)
        scores.append(jnp.maximum(hi1, hi2) + jnp.maximum(jnp.minimum(hi1, hi2), jnp.maximum(lo1, lo2)))
    best = scores[0]
    gidx = jnp.zeros(best.shape, jnp.int32)
    for gi in range(1, N_GROUPS):
        upd = scores[gi] > best
        best = jnp.where(upd, scores[gi], best)
        gidx = jnp.where(upd, gi, gidx)
    a = [jnp.where(gidx == 0, p[j], jnp.where(gidx == 1, p[4 + j], jnp.where(gidx == 2, p[8 + j], p[12 + j])))
         for j in range(GROUP_SIZE)]
    v1 = a[0]
    i1 = jnp.zeros(best.shape, jnp.int32)
    for j in range(1, GROUP_SIZE):
        upd = a[j] > v1
        v1 = jnp.where(upd, a[j], v1)
        i1 = jnp.where(upd, j, i1)
    v2 = jnp.full(best.shape, -1.0, f32)
    i2 = jnp.zeros(best.shape, jnp.int32)
    for j in range(GROUP_SIZE):
        upd = jnp.logical_and(i1 != j, a[j] > v2)
        v2 = jnp.where(upd, a[j], v2)
        i2 = jnp.where(upd, j, i2)
    tot = v1 + v2
    w1 = v1 / tot
    w2 = v2 / tot
    first_low = i1 < i2
    lo = jnp.minimum(i1, i2)
    hi = jnp.maximum(i1, i2)
    pair = lo * 3 - ((lo * (lo - 1)) >> 1) + hi - lo - 1
    cls = gidx * 6 + pair
    return cls, jnp.where(first_low, w1, w2), jnp.where(first_low, w2, w1)


def _post_kernel(a_ref, w_ref, x_ref, g1_ref, lng_ref, lnb_ref, sh2_ref, sc2_ref, wr_ref, br_ref,
                 x1_ref, rows_ref, wts_ref, meta_ref, cnt_ref, cnt_scr):
    i = pl.program_id(0)
    tm = x_ref.shape[0]

    @pl.when(i == 0)
    def _():
        cnt_scr[...] = jnp.zeros_like(cnt_scr)

    mix = _mm(_bf(a_ref[...]), w_ref[...])
    x1 = _layer_norm(ALPHA * x_ref[...] + g1_ref[...] * mix, lng_ref[...], lnb_ref[...])
    x1_ref[...] = x1
    xm = x1 * (1.0 + sc2_ref[...]) + sh2_ref[...]
    _store_tokens(rows_ref, xm, 0, SUB)

    x_hi = _bf(xm)
    x_lo = _bf(xm - x_hi.astype(f32))
    wr = wr_ref[...]
    l_hi = _mm_nt(wr, x_hi)
    l_lo = _mm_nt(wr[0:N_EXPERTS], x_lo)
    logits = l_hi[0:N_EXPERTS] + l_hi[N_EXPERTS:2 * N_EXPERTS] + l_lo + br_ref[...]
    mx = jnp.max(logits, axis=0, keepdims=True)
    ex = jnp.exp(logits - mx)
    probs = ex / jnp.sum(ex, axis=0, keepdims=True)
    cls, w_lo, w_hi = _route(probs)

    crow = lax.broadcasted_iota(jnp.int32, (32, tm), 0)
    onehot = crow == cls
    oh = jnp.where(onehot, 1.0, 0.0)
    before = lax.broadcasted_iota(jnp.int32, (tm, tm), 0) < lax.broadcasted_iota(jnp.int32, (tm, tm), 1)
    prefix = _mm(_bf(oh), jnp.where(before, 1.0, 0.0).astype(bf16))
    carry = cnt_scr[...]
    rank = jnp.sum(jnp.where(onehot, prefix + carry, 0.0), axis=0, keepdims=True)
    cnt_new = carry + jnp.sum(oh, axis=1, keepdims=True)
    cnt_scr[...] = cnt_new
    cnt_ref[...] = jnp.broadcast_to(cnt_new, cnt_ref.shape)

    mrow = lax.broadcasted_iota(jnp.int32, (8, tm), 0)
    meta_ref[...] = jnp.where(mrow == 0, cls, jnp.where(mrow == 1, rank.astype(jnp.int32), 0))

    lane = lax.broadcasted_iota(jnp.int32, (tm, LANES), 1)
    wts_ref[...] = jnp.where(lane == 0, _row_to_col(w_lo), jnp.where(lane == 1, _row_to_col(w_hi), 0.0))


def _post(a, w_o, x, g1, lng, lnb, sh2, sc2, tiles_per_block, tm, wr_t, br_col):
    n = x.shape[0]
    mod = lambda arr: _mod_spec(arr, tiles_per_block)
    return pl.pallas_call(
        _post_kernel,
        grid=(n // tm,),
        in_specs=[_row_spec(tm, D), _const_spec(w_o.shape), _row_spec(tm, D), mod(g1),
                  _const_spec(lng.shape), _const_spec(lnb.shape), mod(sh2), mod(sc2),
                  _const_spec(wr_t.shape), _const_spec(br_col.shape)],
        out_specs=[_row_spec(tm, D), _row_spec(tm * SUB, LANES), _row_spec(tm, LANES),
                   pl.BlockSpec((8, tm), lambda i: (0, i)),
                   pl.BlockSpec((32, LANES), lambda i: (0, 0))],
        out_shape=[jax.ShapeDtypeStruct((n, D), f32), jax.ShapeDtypeStruct((n * SUB, LANES), f32),
                   jax.ShapeDtypeStruct((n, LANES), f32), jax.ShapeDtypeStruct((8, n), jnp.int32),
                   jax.ShapeDtypeStruct((32, LANES), f32)],
        scratch_shapes=[pltpu.VMEM((32, 1), f32)],
        compiler_params=_params("arbitrary"),
        name="post",
    )(a, w_o, x, g1, lng, lnb, sh2, sc2, wr_t, br_col)


def _permute_kernel(pos_ref, src_ref, *rest, n, rows, scatter):
    dst_ref, sem = rest[-2], rest[-1]

    def row_copy(i):
        here = pl.ds(pl.multiple_of(i * rows, rows), rows)
        there = pl.ds(pl.multiple_of(pos_ref[i] * rows, rows), rows)
        if scatter:
            return pltpu.make_async_copy(src_ref.at[here], dst_ref.at[there], sem)
        return pltpu.make_async_copy(src_ref.at[there], dst_ref.at[here], sem)

    window = min(DMA_WINDOW, n)

    def issue(i, carry):
        row_copy(i).start()

        @pl.when(i >= window)
        def _():
            row_copy(i - window).wait()
        return carry

    def drain(i, carry):
        row_copy(i).wait()
        return carry

    lax.fori_loop(0, n, issue, 0)
    lax.fori_loop(n - window, n, drain, 0)


def _scatter_rows(pos, src, dst, rows):
    n = pos.shape[0]
    any_spec = pl.BlockSpec(memory_space=pl.ANY)
    return pl.pallas_call(
        functools.partial(_permute_kernel, n=n, rows=rows, scatter=True),
        grid_spec=pltpu.PrefetchScalarGridSpec(
            num_scalar_prefetch=1, grid=(1,),
            in_specs=[any_spec, any_spec], out_specs=any_spec,
            scratch_shapes=[pltpu.SemaphoreType.DMA(())]),
        out_shape=jax.ShapeDtypeStruct(dst.shape, dst.dtype),
        input_output_aliases={2: 0},
        compiler_params=_params("arbitrary"),
        name="scatter_rows",
    )(pos, src, dst)


def _gather_rows(pos, src, rows):
    n = pos.shape[0]
    any_spec = pl.BlockSpec(memory_space=pl.ANY)
    return pl.pallas_call(
        functools.partial(_permute_kernel, n=n, rows=rows, scatter=False),
        grid_spec=pltpu.PrefetchScalarGridSpec(
            num_scalar_prefetch=1, grid=(1,),
            in_specs=[any_spec], out_specs=any_spec,
            scratch_shapes=[pltpu.SemaphoreType.DMA(())]),
        out_shape=jax.ShapeDtypeStruct((n * rows, src.shape[1]), src.dtype),
        compiler_params=_params("arbitrary"),
        name="gather_rows",
    )(pos, src)


def _experts_kernel(elo_ref, ehi_ref, valid_ref, rows_ref,
                    wg_lo, wu_lo, wd_lo, wg_hi, wu_hi, wd_hi, o_ref):
    t = pl.program_id(0)

    @pl.when(valid_ref[t] == 1)
    def _():
        xb = _bf(_load_tokens(rows_ref, TM_E, SUB))

        def ffn(wg, wu, wd):
            hid = _silu(_mm(xb, wg[...])) * _mm(xb, wu[...])
            return _mm(_bf(hid), wd[...])

        _store_tokens(o_ref, ffn(wg_lo, wu_lo, wd_lo), 0, 2 * SUB)
        _store_tokens(o_ref, ffn(wg_hi, wu_hi, wd_hi), SUB, 2 * SUB)

    @pl.when(valid_ref[t] == 0)
    def _():
        o_ref[...] = jnp.zeros_like(o_ref)


def _experts(e_lo, e_hi, valid, rows, wg, wu, wd, layer):
    n_tiles = rows.shape[0] // (TM_E * SUB)
    up = lambda sel: pl.BlockSpec((None, None, D, D_FF), lambda t, lo, hi, v: (layer, sel(lo, hi)[t], 0, 0))
    down = lambda sel: pl.BlockSpec((None, None, D_FF, D), lambda t, lo, hi, v: (layer, sel(lo, hi)[t], 0, 0))
    first = lambda lo, hi: lo
    second = lambda lo, hi: hi
    return pl.pallas_call(
        _experts_kernel,
        grid_spec=pltpu.PrefetchScalarGridSpec(
            num_scalar_prefetch=3, grid=(n_tiles,),
            in_specs=[pl.BlockSpec((TM_E * SUB, LANES), lambda t, lo, hi, v: (t, 0)),
                      up(first), up(first), down(first), up(second), up(second), down(second)],
            out_specs=pl.BlockSpec((TM_E * 2 * SUB, LANES), lambda t, lo, hi, v: (t, 0))),
        out_shape=jax.ShapeDtypeStruct((2 * rows.shape[0], LANES), f32),
        compiler_params=_params("parallel"),
        name="experts",
    )(e_lo, e_hi, valid, rows, wg, wu, wd, wg, wu, wd)


def _moe(rows_p, meta_p, cnt_p, rows_s, meta_s, cnt_s, wg, wu, wd, layer):
    n_p, n_s = meta_p.shape[1], meta_s.shape[1]
    n_tiles = -(-(n_p + n_s) // TM_E) + N_CLASSES
    cp = cnt_p[:N_CLASSES, 0].astype(jnp.int32)
    cs = cnt_s[:N_CLASSES, 0].astype(jnp.int32)
    tiles_c = (cp + cs + TM_E - 1) // TM_E
    tile_end = jnp.cumsum(tiles_c)
    row_off = (tile_end - tiles_c) * TM_E
    pos_p = row_off[meta_p[0]] + meta_p[1]
    pos_s = row_off[meta_s[0]] + cp[meta_s[0]] + meta_s[1]
    t_idx = jnp.arange(n_tiles, dtype=jnp.int32)
    valid = (t_idx < tile_end[-1]).astype(jnp.int32)
    tile_cls = jnp.minimum(jnp.sum((tile_end[None, :] <= t_idx[:, None]).astype(jnp.int32), axis=1), N_CLASSES - 1)
    pairs = [(a, b) for a in range(GROUP_SIZE) for b in range(a + 1, GROUP_SIZE)]
    lo_tab = jnp.asarray([gi * GROUP_SIZE + a for gi in range(N_GROUPS) for a, _ in pairs], jnp.int32)
    hi_tab = jnp.asarray([gi * GROUP_SIZE + b for gi in range(N_GROUPS) for _, b in pairs], jnp.int32)
    e_lo, e_hi = lo_tab[tile_cls], hi_tab[tile_cls]

    buf = jnp.zeros((n_tiles * TM_E * SUB, LANES), f32)
    buf = _scatter_rows(pos_p, rows_p, buf, SUB)
    buf = _scatter_rows(pos_s, rows_s, buf, SUB)
    ff_sorted = _experts(e_lo, e_hi, valid, buf, wg, wu, wd, layer)
    return _gather_rows(pos_p, ff_sorted, 2 * SUB), _gather_rows(pos_s, ff_sorted, 2 * SUB)


def _ffn_out(pair_ref, wts_ref, n):
    wts = wts_ref[...]
    lo = jnp.concatenate([pair_ref[pl.ds(h, n, stride=2 * SUB), :] for h in range(SUB)], axis=1)
    hi = jnp.concatenate([pair_ref[pl.ds(SUB + h, n, stride=2 * SUB), :] for h in range(SUB)], axis=1)
    return wts[:, 0:1] * lo + wts[:, 1:2] * hi


def _mid_kernel(x1_ref, pair_ref, wts_ref, g2_ref, lng_ref, lnb_ref, shkv_ref, sckv_ref, shq_ref, scq_ref,
                wkv_ref, wq_ref, x2_ref, k_ref, v_ref, kb_ref, vb_ref, q_ref):
    ff = _ffn_out(pair_ref, wts_ref, x1_ref.shape[0])
    x2 = _layer_norm(ALPHA * x1_ref[...] + g2_ref[...] * ff, lng_ref[...], lnb_ref[...])
    x2_ref[...] = x2
    xkv = _bf(x2 * (1.0 + sckv_ref[...]) + shkv_ref[...])
    kk = _mm(xkv, wkv_ref[:, 0:D])
    vv = _mm(xkv, wkv_ref[:, D:2 * D])
    _store_tokens(k_ref, kk, 0, SUB)
    _store_tokens(v_ref, vv, 0, SUB)
    kb_ref[...] = _bf(kk)
    vb_ref[...] = _bf(vv)
    xq = _bf(x2 * (1.0 + scq_ref[...]) + shq_ref[...])
    q_ref[...] = _bf(_mm(xq, wq_ref[...]) * (DH_B ** -0.5))


def _mid(x1, pair, wts, g2, lng, lnb, shkv, sckv, shq, scq, tiles_per_block, tm, w_kv, w_q):
    n = x1.shape[0]
    mod = lambda arr: _mod_spec(arr, tiles_per_block)
    sds = lambda dt: jax.ShapeDtypeStruct((n, D), dt)
    kv_sds = jax.ShapeDtypeStruct((n * SUB, LANES), f32)
    return pl.pallas_call(
        _mid_kernel,
        grid=(n // tm,),
        in_specs=[_row_spec(tm, D), _row_spec(tm * 2 * SUB, LANES), _row_spec(tm, LANES), mod(g2),
                  _const_spec(lng.shape), _const_spec(lnb.shape),
                  mod(shkv), mod(sckv), mod(shq), mod(scq), _const_spec(w_kv.shape), _const_spec(w_q.shape)],
        out_specs=[_row_spec(tm, D), _row_spec(tm * SUB, LANES), _row_spec(tm * SUB, LANES),
                   _row_spec(tm, D), _row_spec(tm, D), _row_spec(tm, D)],
        out_shape=[sds(f32), kv_sds, kv_sds, sds(bf16), sds(bf16), sds(bf16)],
        compiler_params=_params("parallel"),
        name="mid",
    )(x1, pair, wts, g2, lng, lnb, shkv, sckv, shq, scq, w_kv, w_q)


def _lambda(lq1_ref, lk1_ref, lq2_ref, lk2_ref, lam_init):
    s1 = jnp.sum(lq1_ref[...] * lk1_ref[...], axis=1, keepdims=True)
    s2 = jnp.sum(lq2_ref[...] * lk2_ref[...], axis=1, keepdims=True)
    return jnp.exp(s1) - jnp.exp(s2) + lam_init


def _online_update(s, v, m_ref, l_ref, acc_ref):
    m_old = m_ref[...]
    m_new = jnp.maximum(m_old, jnp.max(s, axis=1, keepdims=True))
    p = jnp.exp(s - jnp.tile(m_new, (1, s.shape[1] // LANES)))
    corr = jnp.exp(m_old - m_new)
    l_ref[...] = l_ref[...] * corr + jnp.sum(p, axis=1, keepdims=True)
    acc_ref[...] = acc_ref[...] * jnp.tile(corr, (1, acc_ref.shape[1] // LANES)) + _mm(p.astype(v.dtype), v)
    m_ref[...] = m_new


def _dattn_kernel(slope_ref, q_ref, k_ref, v_ref, lq1_ref, lk1_ref, lq2_ref, lk2_ref, subln_ref,
                  o_ref, m_scr, l_scr, acc_scr, *, lam_init):
    t = q_ref.shape[0]
    slope = slope_ref[pl.program_id(1)]
    lam = _lambda(lq1_ref, lk1_ref, lq2_ref, lk2_ref, lam_init)
    first_map = lax.broadcasted_iota(jnp.int32, (1, 2 * DH_B), 1) < DH_B
    rel_q = lax.broadcasted_iota(jnp.int32, (TH, 1), 0)

    def attend(chain, qm, q0, k0, nk, masked):
        s = _mm_nt(qm, k_ref[pl.ds(k0, nk), :])
        rel_k = lax.broadcasted_iota(jnp.int32, (1, nk), 1) + (k0 - q0)
        s = s + slope * rel_k.astype(f32)
        if masked:
            s = jnp.where(rel_k <= rel_q, s, NEG_BIG)
        _online_update(s, v_ref[pl.ds(k0, nk), :], m_scr.at[chain], l_scr.at[chain], acc_scr.at[chain])

    for i in range(t // TQ):
        q0 = i * TQ
        m_scr[...] = jnp.full(m_scr.shape, NEG_BIG, f32)
        l_scr[...] = jnp.zeros_like(l_scr)
        acc_scr[...] = jnp.zeros_like(acc_scr)
        chains = []
        for half in range(2):
            q = q_ref[q0 + half * TH:q0 + (half + 1) * TH, :]
            zero = jnp.zeros_like(q)
            chains.append((2 * half, jnp.where(first_map, q, zero), q0 + half * TH))
            chains.append((2 * half + 1, jnp.where(first_map, zero, q), q0 + half * TH))

        def below_diagonal(j, carry):
            k0 = pl.multiple_of(j * TQ, TQ)
            for chain, qm, row0 in chains:
                attend(chain, qm, row0, k0, TQ, False)
            return carry

        lax.fori_loop(0, i, below_diagonal, 0)
        for chain, qm, row0 in chains:
            if row0 > q0:
                attend(chain, qm, row0, q0, TH, False)
            attend(chain, qm, row0, row0, TH, True)

        for half in range(2):
            o1 = acc_scr[2 * half] / l_scr[2 * half]
            o2 = acc_scr[2 * half + 1] / l_scr[2 * half + 1]
            o = o1 - lam * o2
            o_ref[q0 + half * TH:q0 + (half + 1) * TH, :] = _bf(_rms_norm(o, subln_ref[...]) * (1.0 - lam_init))


def _dattn(slopes, q, k, v, lq1, lk1, lq2, lk2, subln, n_seq, t, lam_init):
    head = pl.BlockSpec((t, 2 * DH_B), lambda b, h: (b, h))
    small = lambda w: pl.BlockSpec((1, w), lambda b, h: (0, 0))
    return pl.pallas_call(
        functools.partial(_dattn_kernel, lam_init=lam_init),
        grid=(n_seq, H_B),
        in_specs=[pl.BlockSpec(memory_space=pltpu.SMEM), head, head, head,
                  small(DH_B), small(DH_B), small(DH_B), small(DH_B), small(DV_B)],
        out_specs=head,
        out_shape=jax.ShapeDtypeStruct((n_seq * t, H_B * DV_B), bf16),
        scratch_shapes=[pltpu.VMEM((4, TH, LANES), f32), pltpu.VMEM((4, TH, LANES), f32),
                        pltpu.VMEM((4, TH, DV_B), f32)],
        compiler_params=_params("parallel", "parallel"),
        name="dattn_prompt",
    )(slopes, q, k, v, lq1, lk1, lq2, lk2, subln)


def _dattn_step_kernel(pt_ref, q_ref, kn_ref, vn_ref, nslope_ref, lq1_ref, lk1_ref, lq2_ref, lk2_ref,
                       subln_ref, *rest, past_len, lam_init):
    n_pg = PAGES_PER_STEP
    k_refs, v_refs = rest[0:n_pg], rest[n_pg:2 * n_pg]
    o_ref, qbd_scr, m_scr, l_scr, acc_scr = rest[2 * n_pg:]
    j = pl.program_id(1)
    rows = 2 * H_B

    @pl.when(j == 0)
    def _():
        lane = lax.broadcasted_iota(jnp.int32, (rows, D), 1)
        rowi = lax.broadcasted_iota(jnp.int32, (rows, D), 0)
        q = jnp.broadcast_to(q_ref[...].astype(f32), (rows, D))
        qbd_scr[...] = jnp.where((lane >> 6) == rowi, q, 0.0)
        m_scr[...] = jnp.full(m_scr.shape, NEG_BIG, f32)
        l_scr[...] = jnp.zeros_like(l_scr)
        acc_scr[...] = jnp.zeros_like(acc_scr)

    qbd = qbd_scr[...]
    for u in range(n_pg):
        s = _mm_nt(qbd, _load_tokens(k_refs[u], PAGE, SUB))
        kpos = (j * n_pg + u) * PAGE + lax.broadcasted_iota(jnp.int32, (1, PAGE), 1)
        s = s + nslope_ref[...] * (past_len - kpos).astype(f32)
        _online_update(s, _load_tokens(v_refs[u], PAGE, SUB), m_scr, l_scr, acc_scr)

    @pl.when(j == pl.num_programs(1) - 1)
    def _():
        kn = jnp.concatenate([kn_ref[h:h + 1, :] for h in range(H_B)], axis=1)
        vn = jnp.concatenate([vn_ref[h:h + 1, :] for h in range(H_B)], axis=1)
        kn_b = _bf(kn).astype(f32)
        vn_b = _bf(vn).astype(f32)
        s = jnp.sum(qbd * kn_b, axis=1, keepdims=True)
        m_old = m_scr[...]
        m_new = jnp.maximum(m_old, s)
        p = jnp.exp(s - m_new)
        corr = jnp.exp(m_old - m_new)
        l_fin = l_scr[...] * corr + p
        acc = acc_scr[...] * jnp.tile(corr, (1, SUB)) + _bf(jnp.tile(p, (1, SUB))).astype(f32) * vn_b
        lane = lax.broadcasted_iota(jnp.int32, (rows, D), 1)
        rowi = lax.broadcasted_iota(jnp.int32, (rows, D), 0)
        a = jnp.where((lane >> 7) == (rowi >> 1), acc / jnp.tile(l_fin, (1, SUB)), 0.0)
        a1 = jnp.sum(jnp.where((rowi & 1) == 0, a, 0.0), axis=0, keepdims=True)
        a2 = jnp.sum(jnp.where((rowi & 1) == 1, a, 0.0), axis=0, keepdims=True)
        o = a1 - _lambda(lq1_ref, lk1_ref, lq2_ref, lk2_ref, lam_init) * a2
        for h in range(H_B):
            hs = slice(h * DV_B, (h + 1) * DV_B)
            o_ref[:, hs] = _rms_norm(o[:, hs], subln_ref[...]) * (1.0 - lam_init)


def _dattn_step(page_table, q, k_new, v_new, cache_k, cache_v, nslope_rows, lq1, lk1, lq2, lk2, subln,
                past_len, lam_init):
    n, n_pages = page_table.shape
    n_pool = cache_k.shape[0]
    ck = cache_k.reshape(n_pool, PAGE * H_B, DV_B)
    cv = cache_v.reshape(n_pool, PAGE * H_B, DV_B)
    n_pg = PAGES_PER_STEP
    vec = pl.BlockSpec((None, 1, D), lambda b, j, pt: (b, 0, 0))
    new_kv = pl.BlockSpec((SUB, LANES), lambda b, j, pt: (b, 0))
    const = lambda shape: pl.BlockSpec(shape, lambda b, j, pt: (0,) * len(shape))

    def page(u):
        return pl.BlockSpec((None, PAGE * H_B, DV_B), lambda b, j, pt: (pt[b, j * n_pg + u], 0, 0))

    pages = [page(u) for u in range(n_pg)]
    out = pl.pallas_call(
        functools.partial(_dattn_step_kernel, past_len=past_len, lam_init=lam_init),
        grid_spec=pltpu.PrefetchScalarGridSpec(
            num_scalar_prefetch=1, grid=(n, n_pages // n_pg),
            in_specs=[vec, new_kv, new_kv, const((2 * H_B, 1)), const((1, DH_B)), const((1, DH_B)),
                      const((1, DH_B)), const((1, DH_B)), const((1, DV_B))] + pages + pages,
            out_specs=vec,
            scratch_shapes=[pltpu.VMEM((2 * H_B, D), f32), pltpu.VMEM((2 * H_B, LANES), f32),
                            pltpu.VMEM((2 * H_B, LANES), f32), pltpu.VMEM((2 * H_B, D), f32)]),
        out_shape=jax.ShapeDtypeStruct((n, 1, D), f32),
        compiler_params=_params("parallel", "arbitrary"),
        name="dattn_step",
    )(page_table, q.reshape(n, 1, D), k_new, v_new, nslope_rows,
      lq1, lk1, lq2, lk2, subln, *([ck] * n_pg), *([cv] * n_pg))
    return out.reshape(n, D)


def _final_kernel(x_ref, pair_ref, wts_ref, g2_ref, lng_ref, lnb_ref, y_ref):
    ff = _ffn_out(pair_ref, wts_ref, x_ref.shape[0])
    y_ref[...] = _layer_norm(ALPHA * x_ref[...] + g2_ref[...] * ff, lng_ref[...], lnb_ref[...])


def _final(x, pair, wts, g2, lng, lnb, tiles_per_block, tm):
    n = x.shape[0]
    return pl.pallas_call(
        _final_kernel,
        grid=(n // tm,),
        in_specs=[_row_spec(tm, D), _row_spec(tm * 2 * SUB, LANES), _row_spec(tm, LANES),
                  _mod_spec(g2, tiles_per_block), _const_spec(lng.shape), _const_spec(lnb.shape)],
        out_specs=_row_spec(tm, D),
        out_shape=jax.ShapeDtypeStruct((n, D), f32),
        compiler_params=_params("parallel"),
        name="final",
    )(x, pair, wts, g2, lng, lnb)


def kernel(x_prompt, x_sample, c_prompt, c_sample, state_gla, cache_k, cache_v, page_table, w_ada, b_ada,
           ln_mix_g, ln_mix_b, ln_ffn_g, ln_ffn_b, w_in_a, w_gate_up_a, b_gate_a, gn_a, w_o_a, w_ada_kv,
           b_ada_kv, w_kv, w_q_b, lam_q1, lam_k1, lam_q2, lam_k2, subln_b, w_o_b, w_router, b_router,
           w_exp_gate, w_exp_up, w_exp_down):
    n_seq, t, _ = x_prompt.shape
    n_smp = x_sample.shape[0]
    n_p = n_seq * t
    past_len = page_table.shape[1] * PAGE
    xp = x_prompt.reshape(n_p, D)
    xs = x_sample.reshape(n_smp, D)
    tpb = t // TM

    c_all = jnp.concatenate([c_prompt, c_sample], axis=0)
    ada = _ada(c_all, w_ada, b_ada.reshape(DEPTH, 1, 6 * D))
    ada_kv = _ada(c_all, w_ada_kv.reshape(1, D, 2 * D), b_ada_kv.reshape(1, 1, 2 * D))[0]

    def mods(table, idx):
        part = table[:, idx * D:(idx + 1) * D]
        return part[:n_seq].reshape(n_seq, 1, D), part[n_seq:].reshape(1, n_smp, D)

    row = lambda a: a.reshape(1, -1)

    w_in = w_in_a[0]
    w_main = _bf(w_in[:, :2 * QK_A + 2 * VW_A])
    w_gd = _bf(jnp.pad(w_in[:, 2 * QK_A + 2 * VW_A:], ((0, 0), (0, LANES - GATE_RANK))))
    w_gu = _bf(jnp.pad(w_gate_up_a[0], ((0, LANES - GATE_RANK), (0, 0))))
    wr_t = w_router.T
    wr_hi = _bf(wr_t)
    wr_split = jnp.concatenate([wr_hi, _bf(wr_t - wr_hi.astype(f32))], axis=0)
    br_col = b_router.reshape(N_EXPERTS, 1)
    wg, wu, wd = _bf(w_exp_gate), _bf(w_exp_up), _bf(w_exp_down)

    sh1, sc1, g1, sh2, sc2, g2 = [mods(ada[0], i) for i in range(6)]
    b_g = row(b_gate_a[0])
    qp, kp, vp, rp, gp = _proj_a(xp, sh1[0], sc1[0], tpb, TM, w_main, w_gd, w_gu, b_g)
    qs, ks, vs, rs, gs = _proj_a(xs, sh1[1], sc1[1], 1, n_smp, w_main, w_gd, w_gu, b_g)
    gn = row(gn_a[0])
    ap, s_prompt = _gla_chunk(qp, kp, vp, rp, gp, gn, n_seq, t)
    as_, s_sample = _gla_step(qs, ks, vs, rs, gs, gn, state_gla[0])

    w_oa = _bf(w_o_a[0])
    lng, lnb = row(ln_mix_g[0]), row(ln_mix_b[0])
    x1p, rows_p, wts_p, meta_p, cnt_p = _post(ap, w_oa, xp, g1[0], lng, lnb, sh2[0], sc2[0], tpb, TM,
                                              wr_split, br_col)
    x1s, rows_s, wts_s, meta_s, cnt_s = _post(as_, w_oa, xs, g1[1], lng, lnb, sh2[1], sc2[1], 1, n_smp,
                                              wr_split, br_col)
    ffp, ffs = _moe(rows_p, meta_p, cnt_p, rows_s, meta_s, cnt_s, wg, wu, wd, 0)

    shkv, sckv = mods(ada_kv, 0), mods(ada_kv, 1)
    sh1, sc1, g1b, sh2b, sc2b, g2b = [mods(ada[1], i) for i in range(6)]
    lng, lnb = row(ln_ffn_g[0]), row(ln_ffn_b[0])
    w_kvb, w_qb = _bf(w_kv), _bf(w_q_b[0])
    x2p, k_p, v_p, kb_p, vb_p, q_p = _mid(x1p, ffp, wts_p, g2[0], lng, lnb, shkv[0], sckv[0], sh1[0], sc1[0],
                                          tpb, TM, w_kvb, w_qb)
    x2s, k_s, v_s, _, _, q_s = _mid(x1s, ffs, wts_s, g2[1], lng, lnb, shkv[1], sckv[1], sh1[1], sc1[1],
                                    1, n_smp, w_kvb, w_qb)

    lam_init = 0.8 - 0.6 * math.exp(-0.3 * 1)
    slopes = jnp.asarray([2.0 ** (-8.0 * (i + 1) / H_B) for i in range(H_B)], f32)
    nslope_rows = -jnp.repeat(slopes, 2).reshape(2 * H_B, 1)
    lam_args = (row(lam_q1[0]), row(lam_k1[0]), row(lam_q2[0]), row(lam_k2[0]), row(subln_b[0]))
    bp = _dattn(slopes, q_p, kb_p, vb_p, *lam_args, n_seq, t, lam_init)
    bs = _dattn_step(page_table, q_s, k_s, v_s, cache_k, cache_v, nslope_rows, *lam_args, past_len, lam_init)

    w_ob = _bf(w_o_b[0])
    lng, lnb = row(ln_mix_g[1]), row(ln_mix_b[1])
    x3p, rows_p, wts_p, meta_p, cnt_p = _post(bp, w_ob, x2p, g1b[0], lng, lnb, sh2b[0], sc2b[0], tpb, TM,
                                              wr_split, br_col)
    x3s, rows_s, wts_s, meta_s, cnt_s = _post(bs, w_ob, x2s, g1b[1], lng, lnb, sh2b[1], sc2b[1], 1, n_smp,
                                              wr_split, br_col)
    ffp, ffs = _moe(rows_p, meta_p, cnt_p, rows_s, meta_s, cnt_s, wg, wu, wd, 1)
    lng, lnb = row(ln_ffn_g[1]), row(ln_ffn_b[1])
    y_p = _final(x3p, ffp, wts_p, g2b[0], lng, lnb, tpb, TM)
    y_s = _final(x3s, ffs, wts_s, g2b[1], lng, lnb, 1, n_smp)

    n_pg_p = t // PAGE
    return (y_p.reshape(n_seq, t, D), y_s.reshape(n_smp, 1, D),
            s_prompt[None], s_sample[None],
            k_p.reshape(n_seq, n_pg_p, PAGE, H_B, 2 * DH_B), v_p.reshape(n_seq, n_pg_p, PAGE, H_B, DV_B),
            k_s.reshape(n_smp, 1, H_B, 2 * DH_B), v_s.reshape(n_smp, 1, H_B, DV_B))
```

```python
import functools
import math

import jax
import jax.numpy as jnp
from jax import lax
from jax.experimental import pallas as pl
from jax.experimental.pallas import tpu as pltpu

f32 = jnp.float32
bf16 = jnp.bfloat16

D = 1024
H_A, DK_A, DV_A = 4, 128, 256
QK_A, VW_A = H_A * DK_A, H_A * DV_A
GATE_RANK = 16
GATE_TAU = 16.0
H_B, DH_B, DV_B = 8, 64, 128
N_EXPERTS, N_GROUPS, GROUP_SIZE = 16, 4, 4
D_FF = 512
N_CLASSES = N_GROUPS * 6
DEPTH = 2
ALPHA = (2.0 * DEPTH) ** 0.25
LN_EPS = 1e-5
RMS_EPS = 1e-6
NEG_BIG = -1e30
PAGE = 128

LANES = 128
VMEM_LIMIT = 48 * 1024 * 1024
TM = 256
TM_E = 256
GLA_C = 64
GLA_SEQS = 4
TQ = 512
TH = TQ // 2
PAGES_PER_STEP = 8
SUB = 8
assert D == SUB * LANES


def _load_tokens(ref, n, rows_per_token):
    return jnp.concatenate([ref[pl.ds(h, n, stride=rows_per_token), :] for h in range(SUB)], axis=1)


def _store_tokens(ref, val, offset, rows_per_token):
    n = val.shape[0]
    for h in range(SUB):
        ref[pl.ds(offset + h, n, stride=rows_per_token), :] = val[:, h * LANES:(h + 1) * LANES]


def _bf(a):
    return a.astype(bf16)


def _mm(a, b):
    return jnp.dot(a, b, preferred_element_type=f32)


def _mm_nt(a, b):
    return lax.dot_general(a, b, (((1,), (1,)), ((), ())), preferred_element_type=f32)


def _mm_tn(a, b):
    return lax.dot_general(a, b, (((0,), (0,)), ((), ())), preferred_element_type=f32)


def _params(*sem):
    return pltpu.CompilerParams(dimension_semantics=sem, vmem_limit_bytes=VMEM_LIMIT)


def _layer_norm(h, g, b):
    mu = jnp.mean(h, axis=-1, keepdims=True)
    hc = h - mu
    var = jnp.mean(hc * hc, axis=-1, keepdims=True)
    return hc * lax.rsqrt(var + LN_EPS) * g + b


def _rms_norm(o, g):
    ms = jnp.mean(o * o, axis=-1, keepdims=True)
    return o * lax.rsqrt(ms + RMS_EPS) * g


def _silu(x):
    return x / (1.0 + jnp.exp(-x))


def _row_to_col(row):
    n = row.shape[1]
    eye = lax.broadcasted_iota(jnp.int32, (n, n), 0) == lax.broadcasted_iota(jnp.int32, (n, n), 1)
    return jnp.sum(jnp.where(eye, row, 0.0), axis=1, keepdims=True)


def _ada_kernel(c_ref, w_ref, b_ref, o_ref):
    o_ref[...] = _mm(_bf(c_ref[...]), _bf(w_ref[...])) + b_ref[...]


def _ada(c_all, w3, b3, tn=1024):
    n_l, _, n = w3.shape
    m = c_all.shape[0]
    return pl.pallas_call(
        _ada_kernel,
        grid=(n_l, n // tn),
        in_specs=[pl.BlockSpec((m, D), lambda l, j: (0, 0)),
                  pl.BlockSpec((None, D, tn), lambda l, j: (l, 0, j)),
                  pl.BlockSpec((None, 1, tn), lambda l, j: (l, 0, j))],
        out_specs=pl.BlockSpec((None, m, tn), lambda l, j: (l, 0, j)),
        out_shape=jax.ShapeDtypeStruct((n_l, m, n), f32),
        compiler_params=_params("parallel", "parallel"),
        name="ada",
    )(c_all, w3, b3)


def _mod_spec(mod, tiles_per_block):
    return pl.BlockSpec((None, mod.shape[1], D), lambda i, *_: (i // tiles_per_block, 0, 0))


def _row_spec(tm, width):
    return pl.BlockSpec((tm, width), lambda i, *_: (i, 0))


def _const_spec(shape):
    nd = len(shape)
    return pl.BlockSpec(shape, lambda i, *_: (0,) * nd)


def _proj_a_kernel(x_ref, sh_ref, sc_ref, w_ref, wgd_ref, wgu_ref, bg_ref,
                   q_ref, k_ref, v_ref, r_ref, g_ref):
    xm = _bf(x_ref[...] * (1.0 + sc_ref[...]) + sh_ref[...])
    q_ref[...] = _mm(xm, w_ref[:, 0:QK_A]) * (DK_A ** -0.5)
    k_ref[...] = _mm(xm, w_ref[:, QK_A:2 * QK_A])
    v_ref[...] = _mm(xm, w_ref[:, 2 * QK_A:2 * QK_A + VW_A])
    r_ref[...] = _mm(xm, w_ref[:, 2 * QK_A + VW_A:2 * QK_A + 2 * VW_A])
    gd = _mm(xm, wgd_ref[...])
    z = _mm(_bf(gd), wgu_ref[...]) + bg_ref[...]
    g_ref[...] = (jnp.minimum(z, 0.0) - jnp.log1p(jnp.exp(-jnp.abs(z)))) * (1.0 / GATE_TAU)


def _proj_a(x, sh, sc, tiles_per_block, tm, w_main, w_gd, w_gu, b_g):
    n = x.shape[0]
    outs = [jax.ShapeDtypeStruct((n, QK_A), f32), jax.ShapeDtypeStruct((n, QK_A), f32),
            jax.ShapeDtypeStruct((n, VW_A), f32), jax.ShapeDtypeStruct((n, VW_A), f32),
            jax.ShapeDtypeStruct((n, QK_A), f32)]
    return pl.pallas_call(
        _proj_a_kernel,
        grid=(n // tm,),
        in_specs=[_row_spec(tm, D), _mod_spec(sh, tiles_per_block), _mod_spec(sc, tiles_per_block),
                  _const_spec(w_main.shape), _const_spec(w_gd.shape), _const_spec(w_gu.shape),
                  _const_spec(b_g.shape)],
        out_specs=[_row_spec(tm, QK_A), _row_spec(tm, QK_A), _row_spec(tm, VW_A),
                   _row_spec(tm, VW_A), _row_spec(tm, QK_A)],
        out_shape=outs,
        compiler_params=_params("parallel"),
        name="proj_a",
    )(x, sh, sc, w_main, w_gd, w_gu, b_g)


def _gla_chunk_kernel(q_ref, k_ref, v_ref, r_ref, g_ref, gn_ref, o_ref, s_ref, st_scr, *, n_chunks):
    c = pl.program_id(1)
    C = q_ref.shape[1]

    @pl.when(c == 0)
    def _():
        st_scr[...] = jnp.zeros_like(st_scr)

    row = lax.broadcasted_iota(jnp.int32, (C, C), 0)
    col = lax.broadcasted_iota(jnp.int32, (C, C), 1)
    tril = col <= row
    tril_b = jnp.where(tril, 1.0, 0.0).astype(bf16)
    mid = C // 2 - 1
    for b in range(q_ref.shape[0]):
        g = g_ref[b]
        g1 = _bf(g)
        rem = g - g1.astype(f32)
        g2 = _bf(rem)
        g3 = _bf(rem - g2.astype(f32))
        cum = _mm(tril_b, g1) + _mm(tril_b, g2) + _mm(tril_b, g3)
        for h in range(H_A):
            ks = slice(h * DK_A, (h + 1) * DK_A)
            vs = slice(h * DV_A, (h + 1) * DV_A)
            cum_h = cum[:, ks]
            m = cum_h[mid:mid + 1, :]
            last = cum_h[C - 1:C, :]
            qh = q_ref[b, :, ks]
            kh = k_ref[b, :, ks]
            vb = _bf(v_ref[b, :, vs])
            q_in = _bf(qh * jnp.exp(cum_h))
            q_mid = _bf(qh * jnp.exp(cum_h - m))
            k_mid = _bf(kh * jnp.exp(m - cum_h))
            k_last = _bf(kh * jnp.exp(last - cum_h))
            st = st_scr[b, h]
            inter = _mm_nt(q_in, _bf(st))
            attn = jnp.where(tril, _mm_nt(q_mid, k_mid), 0.0)
            o = inter + _mm(_bf(attn), vb)
            st_new = st * jnp.exp(last) + _mm_tn(vb, k_last)
            st_scr[b, h] = st_new
            rh = r_ref[b, :, vs]
            o_ref[b, :, vs] = _bf(_rms_norm(o, gn_ref[...]) * _silu(rh))

    @pl.when(c == n_chunks - 1)
    def _():
        for b in range(q_ref.shape[0]):
            for h in range(H_A):
                s_ref[b, h] = st_scr[b, h].T


def _gla_chunk(q, k, v, r, g, gn, n_seq, t):
    n_chunks = t // GLA_C
    nb = GLA_SEQS
    rows = lambda w: pl.BlockSpec((nb, GLA_C, w), lambda b, c: (b, c, 0))
    seq = lambda a: a.reshape(n_seq, t, a.shape[1])
    o, s_fin = pl.pallas_call(
        functools.partial(_gla_chunk_kernel, n_chunks=n_chunks),
        grid=(n_seq // nb, n_chunks),
        in_specs=[rows(QK_A), rows(QK_A), rows(VW_A), rows(VW_A), rows(QK_A),
                  pl.BlockSpec((1, DV_A), lambda b, c: (0, 0))],
        out_specs=[rows(VW_A), pl.BlockSpec((nb, H_A, DK_A, DV_A), lambda b, c: (b, 0, 0, 0))],
        out_shape=[jax.ShapeDtypeStruct((n_seq, t, VW_A), bf16),
                   jax.ShapeDtypeStruct((n_seq, H_A, DK_A, DV_A), f32)],
        scratch_shapes=[pltpu.VMEM((nb, H_A, DV_A, DK_A), f32)],
        compiler_params=_params("parallel", "arbitrary"),
        name="gla_chunk",
    )(seq(q), seq(k), seq(v), seq(r), seq(g), gn)
    return o.reshape(n_seq * t, VW_A), s_fin


def _gla_step_kernel(q_ref, k_ref, v_ref, r_ref, g_ref, gn_ref, s_ref, o_ref, sn_ref):
    for h in range(H_A):
        ks = slice(h * DK_A, (h + 1) * DK_A)
        vs = slice(h * DV_A, (h + 1) * DV_A)
        q_col = _row_to_col(q_ref[:, ks])
        k_col = _row_to_col(k_ref[:, ks])
        eg_col = _row_to_col(jnp.exp(g_ref[:, ks]))
        s_new = eg_col * s_ref[h] + k_col * v_ref[:, vs]
        sn_ref[h] = s_new
        o = jnp.sum(q_col * s_new, axis=0, keepdims=True)
        rh = r_ref[:, vs]
        o_ref[:, vs] = _rms_norm(o, gn_ref[...]) * _silu(rh)


def _gla_step(q, k, v, r, g, gn, state):
    n = q.shape[0]
    vec = lambda w: pl.BlockSpec((None, 1, w), lambda b: (b, 0, 0))
    st = pl.BlockSpec((None, H_A, DK_A, DV_A), lambda b: (b, 0, 0, 0))
    o, s_new = pl.pallas_call(
        _gla_step_kernel,
        grid=(n,),
        in_specs=[vec(QK_A), vec(QK_A), vec(VW_A), vec(VW_A), vec(QK_A),
                  pl.BlockSpec((1, DV_A), lambda b: (0, 0)), st],
        out_specs=[vec(VW_A), st],
        out_shape=[jax.ShapeDtypeStruct((n, 1, VW_A), f32),
                   jax.ShapeDtypeStruct((n, H_A, DK_A, DV_A), f32)],
        compiler_params=_params("parallel"),
        name="gla_step",
    )(q.reshape(n, 1, QK_A), k.reshape(n, 1, QK_A), v.reshape(n, 1, VW_A), r.reshape(n, 1, VW_A),
      g.reshape(n, 1, QK_A), gn, state)
    return o.reshape(n, VW_A), s_new


def _route(probs):
    p = [probs[e:e + 1, :] for e in range(N_EXPERTS)]
    scores = []
    for gi in range(N_GROUPS):
        a0, a1, a2, a3 = p[4 * gi:4 * gi + 4]
        hi1, lo1 = jnp.maximum(a0, a1), jnp.minimum(a0, a1)
        hi2, lo2 = jnp.maximum(a2, a3), jnp.minimum(a2, a3)
        scores.append(jnp.maximum(hi1, hi2) + jnp.maximum(jnp.minimum(hi1, hi2), jnp.maximum(lo1, lo2)))
    best = scores[0]
    gidx = jnp.zeros(best.shape, jnp.int32)
    for gi in range(1, N_GROUPS):
        upd = scores[gi] > best
        best = jnp.where(upd, scores[gi], best)
        gidx = jnp.where(upd, gi, gidx)
    a = [jnp.where(gidx == 0, p[j], jnp.where(gidx == 1, p[4 + j], jnp.where(gidx == 2, p[8 + j], p[12 + j])))
         for j in range(GROUP_SIZE)]
    v1 = a[0]
    i1 = jnp.zeros(best.shape, jnp.int32)
    for j in range(1, GROUP_SIZE):
        upd = a[j] > v1
        v1 = jnp.where(upd, a[j], v1)
        i1 = jnp.where(upd, j, i1)
    v2 = jnp.full(best.shape, -1.0, f32)
    i2 = jnp.zeros(best.shape, jnp.int32)
    for j in range(GROUP_SIZE):
        upd = jnp.logical_and(i1 != j, a[j] > v2)
        v2 = jnp.where(upd, a[j], v2)
        i2 = jnp.where(upd, j, i2)
    tot = v1 + v2
    w1 = v1 / tot
    w2 = v2 / tot
    first_low = i1 < i2
    lo = jnp.minimum(i1, i2)
    hi = jnp.maximum(i1, i2)
    pair = lo * 3 - ((lo * (lo - 1)) >> 1) + hi - lo - 1
    cls = gidx * 6 + pair
    return cls, jnp.where(first_low, w1, w2), jnp.where(first_low, w2, w1)


def _post_kernel(a_ref, w_ref, x_ref, g1_ref, lng_ref, lnb_ref, sh2_ref, sc2_ref, wr_ref, br_ref,
                 x1_ref, rows_ref, wts_ref, meta_ref, cnt_ref, cnt_scr):
    i = pl.program_id(0)
    tm = x_ref.shape[0]

    @pl.when(i == 0)
    def _():
        cnt_scr[...] = jnp.zeros_like(cnt_scr)

    mix = _mm(_bf(a_ref[...]), w_ref[...])
    x1 = _layer_norm(ALPHA * x_ref[...] + g1_ref[...] * mix, lng_ref[...], lnb_ref[...])
    x1_ref[...] = x1
    xm = x1 * (1.0 + sc2_ref[...]) + sh2_ref[...]
    _store_tokens(rows_ref, xm, 0, SUB)

    x_hi = _bf(xm)
    x_lo = _bf(xm - x_hi.astype(f32))
    wr = wr_ref[...]
    l_hi = _mm_nt(wr, x_hi)
    l_lo = _mm_nt(wr[0:N_EXPERTS], x_lo)
    logits = l_hi[0:N_EXPERTS] + l_hi[N_EXPERTS:2 * N_EXPERTS] + l_lo + br_ref[...]
    mx = jnp.max(logits, axis=0, keepdims=True)
    ex = jnp.exp(logits - mx)
    probs = ex / jnp.sum(ex, axis=0, keepdims=True)
    cls, w_lo, w_hi = _route(probs)

    crow = lax.broadcasted_iota(jnp.int32, (32, tm), 0)
    onehot = crow == cls
    oh = jnp.where(onehot, 1.0, 0.0)
    before = lax.broadcasted_iota(jnp.int32, (tm, tm), 0) < lax.broadcasted_iota(jnp.int32, (tm, tm), 1)
    prefix = _mm(_bf(oh), jnp.where(before, 1.0, 0.0).astype(bf16))
    carry = cnt_scr[...]
    rank = jnp.sum(jnp.where(onehot, prefix + carry, 0.0), axis=0, keepdims=True)
    cnt_new = carry + jnp.sum(oh, axis=1, keepdims=True)
    cnt_scr[...] = cnt_new
    cnt_ref[...] = jnp.broadcast_to(cnt_new, cnt_ref.shape)

    mrow = lax.broadcasted_iota(jnp.int32, (8, tm), 0)
    meta_ref[...] = jnp.where(mrow == 0, cls, jnp.where(mrow == 1, rank.astype(jnp.int32), 0))

    lane = lax.broadcasted_iota(jnp.int32, (tm, LANES), 1)
    wts_ref[...] = jnp.where(lane == 0, _row_to_col(w_lo), jnp.where(lane == 1, _row_to_col(w_hi), 0.0))


def _post(a, w_o, x, g1, lng, lnb, sh2, sc2, tiles_per_block, tm, wr_t, br_col):
    n = x.shape[0]
    mod = lambda arr: _mod_spec(arr, tiles_per_block)
    return pl.pallas_call(
        _post_kernel,
        grid=(n // tm,),
        in_specs=[_row_spec(tm, D), _const_spec(w_o.shape), _row_spec(tm, D), mod(g1),
                  _const_spec(lng.shape), _const_spec(lnb.shape), mod(sh2), mod(sc2),
                  _const_spec(wr_t.shape), _const_spec(br_col.shape)],
        out_specs=[_row_spec(tm, D), _row_spec(tm * SUB, LANES), _row_spec(tm, LANES),
                   pl.BlockSpec((8, tm), lambda i: (0, i)),
                   pl.BlockSpec((32, LANES), lambda i: (0, 0))],
        out_shape=[jax.ShapeDtypeStruct((n, D), f32), jax.ShapeDtypeStruct((n * SUB, LANES), f32),
                   jax.ShapeDtypeStruct((n, LANES), f32), jax.ShapeDtypeStruct((8, n), jnp.int32),
                   jax.ShapeDtypeStruct((32, LANES), f32)],
        scratch_shapes=[pltpu.VMEM((32, 1), f32)],
        compiler_params=_params("arbitrary"),
        name="post",
    )(a, w_o, x, g1, lng, lnb, sh2, sc2, wr_t, br_col)


def _scatter_kernel(pos_ref, src_ref, dst_in_ref, dst_ref, sem, *, tm, rows):
    del dst_in_ref
    base = pl.program_id(0) * tm

    def token_copy(r):
        slot = pl.multiple_of(pos_ref[base + r] * rows, rows)
        return pltpu.make_async_copy(src_ref.at[pl.ds(pl.multiple_of(r * rows, rows), rows)],
                                     dst_ref.at[pl.ds(slot, rows)], sem)

    def start(r, carry):
        token_copy(r).start()
        return carry

    def wait(r, carry):
        token_copy(r).wait()
        return carry

    lax.fori_loop(0, tm, start, 0)
    lax.fori_loop(0, tm, wait, 0)


def _scatter_rows(pos, src, dst, rows):
    n = pos.shape[0]
    tm = min(TM, n)
    return pl.pallas_call(
        functools.partial(_scatter_kernel, tm=tm, rows=rows),
        grid_spec=pltpu.PrefetchScalarGridSpec(
            num_scalar_prefetch=1, grid=(n // tm,),
            in_specs=[pl.BlockSpec((tm * rows, LANES), lambda t, pos: (t, 0)),
                      pl.BlockSpec(memory_space=pl.ANY)],
            out_specs=pl.BlockSpec(memory_space=pl.ANY),
            scratch_shapes=[pltpu.SemaphoreType.DMA(())]),
        out_shape=jax.ShapeDtypeStruct(dst.shape, dst.dtype),
        input_output_aliases={2: 0},
        compiler_params=_params("arbitrary"),
        name="scatter_rows",
    )(pos, src, dst)


def _gather_kernel(pos_ref, src_ref, o_ref, sem, *, tm, rows):
    base = pl.program_id(0) * tm

    def token_copy(r):
        slot = pl.multiple_of(pos_ref[base + r] * rows, rows)
        return pltpu.make_async_copy(src_ref.at[pl.ds(slot, rows)],
                                     o_ref.at[pl.ds(pl.multiple_of(r * rows, rows), rows)], sem)

    def start(r, carry):
        token_copy(r).start()
        return carry

    def wait(r, carry):
        token_copy(r).wait()
        return carry

    lax.fori_loop(0, tm, start, 0)
    lax.fori_loop(0, tm, wait, 0)


def _gather_rows(pos, src, rows):
    n = pos.shape[0]
    tm = min(TM, n)
    return pl.pallas_call(
        functools.partial(_gather_kernel, tm=tm, rows=rows),
        grid_spec=pltpu.PrefetchScalarGridSpec(
            num_scalar_prefetch=1, grid=(n // tm,),
            in_specs=[pl.BlockSpec(memory_space=pl.ANY)],
            out_specs=pl.BlockSpec((tm * rows, LANES), lambda t, pos: (t, 0)),
            scratch_shapes=[pltpu.SemaphoreType.DMA(())]),
        out_shape=jax.ShapeDtypeStruct((n * rows, src.shape[1]), src.dtype),
        compiler_params=_params("arbitrary"),
        name="gather_rows",
    )(pos, src)


def _experts_kernel(elo_ref, ehi_ref, valid_ref, rows_ref,
                    wg_lo, wu_lo, wd_lo, wg_hi, wu_hi, wd_hi, o_ref):
    t = pl.program_id(0)

    @pl.when(valid_ref[t] == 1)
    def _():
        xb = _bf(_load_tokens(rows_ref, TM_E, SUB))

        def ffn(wg, wu, wd):
            hid = _silu(_mm(xb, wg[...])) * _mm(xb, wu[...])
            return _mm(_bf(hid), wd[...])

        _store_tokens(o_ref, ffn(wg_lo, wu_lo, wd_lo), 0, 2 * SUB)
        _store_tokens(o_ref, ffn(wg_hi, wu_hi, wd_hi), SUB, 2 * SUB)

    @pl.when(valid_ref[t] == 0)
    def _():
        o_ref[...] = jnp.zeros_like(o_ref)


def _experts(e_lo, e_hi, valid, rows, wg, wu, wd, layer):
    n_tiles = rows.shape[0] // (TM_E * SUB)
    up = lambda sel: pl.BlockSpec((None, None, D, D_FF), lambda t, lo, hi, v: (layer, sel(lo, hi)[t], 0, 0))
    down = lambda sel: pl.BlockSpec((None, None, D_FF, D), lambda t, lo, hi, v: (layer, sel(lo, hi)[t], 0, 0))
    first = lambda lo, hi: lo
    second = lambda lo, hi: hi
    return pl.pallas_call(
        _experts_kernel,
        grid_spec=pltpu.PrefetchScalarGridSpec(
            num_scalar_prefetch=3, grid=(n_tiles,),
            in_specs=[pl.BlockSpec((TM_E * SUB, LANES), lambda t, lo, hi, v: (t, 0)),
                      up(first), up(first), down(first), up(second), up(second), down(second)],
            out_specs=pl.BlockSpec((TM_E * 2 * SUB, LANES), lambda t, lo, hi, v: (t, 0))),
        out_shape=jax.ShapeDtypeStruct((2 * rows.shape[0], LANES), f32),
        compiler_params=_params("parallel"),
        name="experts",
    )(e_lo, e_hi, valid, rows, wg, wu, wd, wg, wu, wd)


def _moe(rows_p, meta_p, cnt_p, rows_s, meta_s, cnt_s, wg, wu, wd, layer):
    n_p, n_s = meta_p.shape[1], meta_s.shape[1]
    n_tiles = -(-(n_p + n_s) // TM_E) + N_CLASSES
    cp = cnt_p[:N_CLASSES, 0].astype(jnp.int32)
    cs = cnt_s[:N_CLASSES, 0].astype(jnp.int32)
    tiles_c = (cp + cs + TM_E - 1) // TM_E
    tile_end = jnp.cumsum(tiles_c)
    row_off = (tile_end - tiles_c) * TM_E
    pos_p = row_off[meta_p[0]] + meta_p[1]
    pos_s = row_off[meta_s[0]] + cp[meta_s[0]] + meta_s[1]
    t_idx = jnp.arange(n_tiles, dtype=jnp.int32)
    valid = (t_idx < tile_end[-1]).astype(jnp.int32)
    tile_cls = jnp.minimum(jnp.sum((tile_end[None, :] <= t_idx[:, None]).astype(jnp.int32), axis=1), N_CLASSES - 1)
    pairs = [(a, b) for a in range(GROUP_SIZE) for b in range(a + 1, GROUP_SIZE)]
    lo_tab = jnp.asarray([gi * GROUP_SIZE + a for gi in range(N_GROUPS) for a, _ in pairs], jnp.int32)
    hi_tab = jnp.asarray([gi * GROUP_SIZE + b for gi in range(N_GROUPS) for _, b in pairs], jnp.int32)
    e_lo, e_hi = lo_tab[tile_cls], hi_tab[tile_cls]

    buf = jnp.zeros((n_tiles * TM_E * SUB, LANES), f32)
    buf = _scatter_rows(pos_p, rows_p, buf, SUB)
    buf = _scatter_rows(pos_s, rows_s, buf, SUB)
    ff_sorted = _experts(e_lo, e_hi, valid, buf, wg, wu, wd, layer)
    return _gather_rows(pos_p, ff_sorted, 2 * SUB), _gather_rows(pos_s, ff_sorted, 2 * SUB)


def _ffn_out(pair_ref, wts_ref, n):
    wts = wts_ref[...]
    lo = jnp.concatenate([pair_ref[pl.ds(h, n, stride=2 * SUB), :] for h in range(SUB)], axis=1)
    hi = jnp.concatenate([pair_ref[pl.ds(SUB + h, n, stride=2 * SUB), :] for h in range(SUB)], axis=1)
    return wts[:, 0:1] * lo + wts[:, 1:2] * hi


def _mid_kernel(x1_ref, pair_ref, wts_ref, g2_ref, lng_ref, lnb_ref, shkv_ref, sckv_ref, shq_ref, scq_ref,
                wkv_ref, wq_ref, x2_ref, k_ref, v_ref, kb_ref, vb_ref, q_ref):
    ff = _ffn_out(pair_ref, wts_ref, x1_ref.shape[0])
    x2 = _layer_norm(ALPHA * x1_ref[...] + g2_ref[...] * ff, lng_ref[...], lnb_ref[...])
    x2_ref[...] = x2
    xkv = _bf(x2 * (1.0 + sckv_ref[...]) + shkv_ref[...])
    kk = _mm(xkv, wkv_ref[:, 0:D])
    vv = _mm(xkv, wkv_ref[:, D:2 * D])
    _store_tokens(k_ref, kk, 0, SUB)
    _store_tokens(v_ref, vv, 0, SUB)
    kb_ref[...] = _bf(kk)
    vb_ref[...] = _bf(vv)
    xq = _bf(x2 * (1.0 + scq_ref[...]) + shq_ref[...])
    q_ref[...] = _bf(_mm(xq, wq_ref[...]) * (DH_B ** -0.5))


def _mid(x1, pair, wts, g2, lng, lnb, shkv, sckv, shq, scq, tiles_per_block, tm, w_kv, w_q):
    n = x1.shape[0]
    mod = lambda arr: _mod_spec(arr, tiles_per_block)
    sds = lambda dt: jax.ShapeDtypeStruct((n, D), dt)
    kv_sds = jax.ShapeDtypeStruct((n * SUB, LANES), f32)
    return pl.pallas_call(
        _mid_kernel,
        grid=(n // tm,),
        in_specs=[_row_spec(tm, D), _row_spec(tm * 2 * SUB, LANES), _row_spec(tm, LANES), mod(g2),
                  _const_spec(lng.shape), _const_spec(lnb.shape),
                  mod(shkv), mod(sckv), mod(shq), mod(scq), _const_spec(w_kv.shape), _const_spec(w_q.shape)],
        out_specs=[_row_spec(tm, D), _row_spec(tm * SUB, LANES), _row_spec(tm * SUB, LANES),
                   _row_spec(tm, D), _row_spec(tm, D), _row_spec(tm, D)],
        out_shape=[sds(f32), kv_sds, kv_sds, sds(bf16), sds(bf16), sds(bf16)],
        compiler_params=_params("parallel"),
        name="mid",
    )(x1, pair, wts, g2, lng, lnb, shkv, sckv, shq, scq, w_kv, w_q)


def _lambda(lq1_ref, lk1_ref, lq2_ref, lk2_ref, lam_init):
    s1 = jnp.sum(lq1_ref[...] * lk1_ref[...], axis=1, keepdims=True)
    s2 = jnp.sum(lq2_ref[...] * lk2_ref[...], axis=1, keepdims=True)
    return jnp.exp(s1) - jnp.exp(s2) + lam_init


def _online_update(s, v, m_ref, l_ref, acc_ref):
    m_old = m_ref[...]
    m_new = jnp.maximum(m_old, jnp.max(s, axis=1, keepdims=True))
    p = jnp.exp(s - jnp.tile(m_new, (1, s.shape[1] // LANES)))
    corr = jnp.exp(m_old - m_new)
    l_ref[...] = l_ref[...] * corr + jnp.sum(p, axis=1, keepdims=True)
    acc_ref[...] = acc_ref[...] * jnp.tile(corr, (1, acc_ref.shape[1] // LANES)) + _mm(p.astype(v.dtype), v)
    m_ref[...] = m_new


def _dattn_kernel(slope_ref, q_ref, k_ref, v_ref, lq1_ref, lk1_ref, lq2_ref, lk2_ref, subln_ref,
                  o_ref, m_scr, l_scr, acc_scr, *, lam_init):
    t = q_ref.shape[0]
    slope = slope_ref[pl.program_id(1)]
    lam = _lambda(lq1_ref, lk1_ref, lq2_ref, lk2_ref, lam_init)
    first_map = lax.broadcasted_iota(jnp.int32, (1, 2 * DH_B), 1) < DH_B
    rel_q = lax.broadcasted_iota(jnp.int32, (TH, 1), 0)

    def attend(chain, qm, q0, k0, nk, masked):
        s = _mm_nt(qm, k_ref[pl.ds(k0, nk), :])
        rel_k = lax.broadcasted_iota(jnp.int32, (1, nk), 1) + (k0 - q0)
        s = s + slope * rel_k.astype(f32)
        if masked:
            s = jnp.where(rel_k <= rel_q, s, NEG_BIG)
        _online_update(s, v_ref[pl.ds(k0, nk), :], m_scr.at[chain], l_scr.at[chain], acc_scr.at[chain])

    for i in range(t // TQ):
        q0 = i * TQ
        m_scr[...] = jnp.full(m_scr.shape, NEG_BIG, f32)
        l_scr[...] = jnp.zeros_like(l_scr)
        acc_scr[...] = jnp.zeros_like(acc_scr)
        chains = []
        for half in range(2):
            q = q_ref[q0 + half * TH:q0 + (half + 1) * TH, :]
            zero = jnp.zeros_like(q)
            chains.append((2 * half, jnp.where(first_map, q, zero), q0 + half * TH))
            chains.append((2 * half + 1, jnp.where(first_map, zero, q), q0 + half * TH))

        def below_diagonal(j, carry):
            k0 = pl.multiple_of(j * TQ, TQ)
            for chain, qm, row0 in chains:
                attend(chain, qm, row0, k0, TQ, False)
            return carry

        lax.fori_loop(0, i, below_diagonal, 0)
        for chain, qm, row0 in chains:
            if row0 > q0:
                attend(chain, qm, row0, q0, TH, False)
            attend(chain, qm, row0, row0, TH, True)

        for half in range(2):
            o1 = acc_scr[2 * half] / l_scr[2 * half]
            o2 = acc_scr[2 * half + 1] / l_scr[2 * half + 1]
            o = o1 - lam * o2
            o_ref[q0 + half * TH:q0 + (half + 1) * TH, :] = _bf(_rms_norm(o, subln_ref[...]) * (1.0 - lam_init))


def _dattn(slopes, q, k, v, lq1, lk1, lq2, lk2, subln, n_seq, t, lam_init):
    head = pl.BlockSpec((t, 2 * DH_B), lambda b, h: (b, h))
    small = lambda w: pl.BlockSpec((1, w), lambda b, h: (0, 0))
    return pl.pallas_call(
        functools.partial(_dattn_kernel, lam_init=lam_init),
        grid=(n_seq, H_B),
        in_specs=[pl.BlockSpec(memory_space=pltpu.SMEM), head, head, head,
                  small(DH_B), small(DH_B), small(DH_B), small(DH_B), small(DV_B)],
        out_specs=head,
        out_shape=jax.ShapeDtypeStruct((n_seq * t, H_B * DV_B), bf16),
        scratch_shapes=[pltpu.VMEM((4, TH, LANES), f32), pltpu.VMEM((4, TH, LANES), f32),
                        pltpu.VMEM((4, TH, DV_B), f32)],
        compiler_params=_params("parallel", "parallel"),
        name="dattn_prompt",
    )(slopes, q, k, v, lq1, lk1, lq2, lk2, subln)


def _dattn_step_kernel(pt_ref, q_ref, kn_ref, vn_ref, nslope_ref, lq1_ref, lk1_ref, lq2_ref, lk2_ref,
                       subln_ref, *rest, past_len, lam_init):
    n_pg = PAGES_PER_STEP
    k_refs, v_refs = rest[0:n_pg], rest[n_pg:2 * n_pg]
    o_ref, qbd_scr, m_scr, l_scr, acc_scr = rest[2 * n_pg:]
    j = pl.program_id(1)
    rows = 2 * H_B

    @pl.when(j == 0)
    def _():
        lane = lax.broadcasted_iota(jnp.int32, (rows, D), 1)
        rowi = lax.broadcasted_iota(jnp.int32, (rows, D), 0)
        q = jnp.broadcast_to(q_ref[...].astype(f32), (rows, D))
        qbd_scr[...] = jnp.where((lane >> 6) == rowi, q, 0.0)
        m_scr[...] = jnp.full(m_scr.shape, NEG_BIG, f32)
        l_scr[...] = jnp.zeros_like(l_scr)
        acc_scr[...] = jnp.zeros_like(acc_scr)

    qbd = qbd_scr[...]
    s = jnp.concatenate([_mm_nt(qbd, _load_tokens(k_refs[u], PAGE, SUB)) for u in range(n_pg)], axis=1)
    kpos = j * (n_pg * PAGE) + lax.broadcasted_iota(jnp.int32, (1, n_pg * PAGE), 1)
    s = s + nslope_ref[...] * (past_len - kpos).astype(f32)
    m_old = m_scr[...]
    m_new = jnp.maximum(m_old, jnp.max(s, axis=1, keepdims=True))
    p = jnp.exp(s - jnp.tile(m_new, (1, n_pg)))
    corr = jnp.exp(m_old - m_new)
    l_scr[...] = l_scr[...] * corr + jnp.sum(p, axis=1, keepdims=True)
    pv = _mm(p[:, 0:PAGE], _load_tokens(v_refs[0], PAGE, SUB))
    for u in range(1, n_pg):
        pv = pv + _mm(p[:, u * PAGE:(u + 1) * PAGE], _load_tokens(v_refs[u], PAGE, SUB))
    acc_scr[...] = acc_scr[...] * jnp.tile(corr, (1, SUB)) + pv
    m_scr[...] = m_new

    @pl.when(j == pl.num_programs(1) - 1)
    def _():
        kn = jnp.concatenate([kn_ref[h:h + 1, :] for h in range(H_B)], axis=1)
        vn = jnp.concatenate([vn_ref[h:h + 1, :] for h in range(H_B)], axis=1)
        kn_b = _bf(kn).astype(f32)
        vn_b = _bf(vn).astype(f32)
        s = jnp.sum(qbd * kn_b, axis=1, keepdims=True)
        m_old = m_scr[...]
        m_new = jnp.maximum(m_old, s)
        p = jnp.exp(s - m_new)
        corr = jnp.exp(m_old - m_new)
        l_fin = l_scr[...] * corr + p
        acc = acc_scr[...] * jnp.tile(corr, (1, SUB)) + _bf(jnp.tile(p, (1, SUB))).astype(f32) * vn_b
        lane = lax.broadcasted_iota(jnp.int32, (rows, D), 1)
        rowi = lax.broadcasted_iota(jnp.int32, (rows, D), 0)
        a = jnp.where((lane >> 7) == (rowi >> 1), acc / jnp.tile(l_fin, (1, SUB)), 0.0)
        a1 = jnp.sum(jnp.where((rowi & 1) == 0, a, 0.0), axis=0, keepdims=True)
        a2 = jnp.sum(jnp.where((rowi & 1) == 1, a, 0.0), axis=0, keepdims=True)
        o = a1 - _lambda(lq1_ref, lk1_ref, lq2_ref, lk2_ref, lam_init) * a2
        for h in range(H_B):
            hs = slice(h * DV_B, (h + 1) * DV_B)
            o_ref[:, hs] = _rms_norm(o[:, hs], subln_ref[...]) * (1.0 - lam_init)


def _dattn_step(page_table, q, k_new, v_new, cache_k, cache_v, nslope_rows, lq1, lk1, lq2, lk2, subln,
                past_len, lam_init):
    n, n_pages = page_table.shape
    n_pool = cache_k.shape[0]
    ck = cache_k.reshape(n_pool, PAGE * H_B, DV_B)
    cv = cache_v.reshape(n_pool, PAGE * H_B, DV_B)
    n_pg = PAGES_PER_STEP
    vec = pl.BlockSpec((None, 1, D), lambda b, j, pt: (b, 0, 0))
    new_kv = pl.BlockSpec((SUB, LANES), lambda b, j, pt: (b, 0))
    const = lambda shape: pl.BlockSpec(shape, lambda b, j, pt: (0,) * len(shape))

    def page(u):
        return pl.BlockSpec((None, PAGE * H_B, DV_B), lambda b, j, pt: (pt[b, j * n_pg + u], 0, 0))

    pages = [page(u) for u in range(n_pg)]
    out = pl.pallas_call(
        functools.partial(_dattn_step_kernel, past_len=past_len, lam_init=lam_init),
        grid_spec=pltpu.PrefetchScalarGridSpec(
            num_scalar_prefetch=1, grid=(n, n_pages // n_pg),
            in_specs=[vec, new_kv, new_kv, const((2 * H_B, 1)), const((1, DH_B)), const((1, DH_B)),
                      const((1, DH_B)), const((1, DH_B)), const((1, DV_B))] + pages + pages,
            out_specs=vec,
            scratch_shapes=[pltpu.VMEM((2 * H_B, D), f32), pltpu.VMEM((2 * H_B, LANES), f32),
                            pltpu.VMEM((2 * H_B, LANES), f32), pltpu.VMEM((2 * H_B, D), f32)]),
        out_shape=jax.ShapeDtypeStruct((n, 1, D), f32),
        compiler_params=_params("parallel", "arbitrary"),
        name="dattn_step",
    )(page_table, q.reshape(n, 1, D), k_new, v_new, nslope_rows,
      lq1, lk1, lq2, lk2, subln, *([ck] * n_pg), *([cv] * n_pg))
    return out.reshape(n, D)


def _final_kernel(x_ref, pair_ref, wts_ref, g2_ref, lng_ref, lnb_ref, y_ref):
    ff = _ffn_out(pair_ref, wts_ref, x_ref.shape[0])
    y_ref[...] = _layer_norm(ALPHA * x_ref[...] + g2_ref[...] * ff, lng_ref[...], lnb_ref[...])


def _final(x, pair, wts, g2, lng, lnb, tiles_per_block, tm):
    n = x.shape[0]
    return pl.pallas_call(
        _final_kernel,
        grid=(n // tm,),
        in_specs=[_row_spec(tm, D), _row_spec(tm * 2 * SUB, LANES), _row_spec(tm, LANES),
                  _mod_spec(g2, tiles_per_block), _const_spec(lng.shape), _const_spec(lnb.shape)],
        out_specs=_row_spec(tm, D),
        out_shape=jax.ShapeDtypeStruct((n, D), f32),
        compiler_params=_params("parallel"),
        name="final",
    )(x, pair, wts, g2, lng, lnb)


def kernel(x_prompt, x_sample, c_prompt, c_sample, state_gla, cache_k, cache_v, page_table, w_ada, b_ada,
           ln_mix_g, ln_mix_b, ln_ffn_g, ln_ffn_b, w_in_a, w_gate_up_a, b_gate_a, gn_a, w_o_a, w_ada_kv,
           b_ada_kv, w_kv, w_q_b, lam_q1, lam_k1, lam_q2, lam_k2, subln_b, w_o_b, w_router, b_router,
           w_exp_gate, w_exp_up, w_exp_down):
    n_seq, t, _ = x_prompt.shape
    n_smp = x_sample.shape[0]
    n_p = n_seq * t
    past_len = page_table.shape[1] * PAGE
    xp = x_prompt.reshape(n_p, D)
    xs = x_sample.reshape(n_smp, D)
    tpb = t // TM

    c_all = jnp.concatenate([c_prompt, c_sample], axis=0)
    ada = _ada(c_all, w_ada, b_ada.reshape(DEPTH, 1, 6 * D))
    ada_kv = _ada(c_all, w_ada_kv.reshape(1, D, 2 * D), b_ada_kv.reshape(1, 1, 2 * D))[0]

    def mods(table, idx):
        part = table[:, idx * D:(idx + 1) * D]
        return part[:n_seq].reshape(n_seq, 1, D), part[n_seq:].reshape(1, n_smp, D)

    row = lambda a: a.reshape(1, -1)

    w_in = w_in_a[0]
    w_main = _bf(w_in[:, :2 * QK_A + 2 * VW_A])
    w_gd = _bf(jnp.pad(w_in[:, 2 * QK_A + 2 * VW_A:], ((0, 0), (0, LANES - GATE_RANK))))
    w_gu = _bf(jnp.pad(w_gate_up_a[0], ((0, LANES - GATE_RANK), (0, 0))))
    wr_t = w_router.T
    wr_hi = _bf(wr_t)
    wr_split = jnp.concatenate([wr_hi, _bf(wr_t - wr_hi.astype(f32))], axis=0)
    br_col = b_router.reshape(N_EXPERTS, 1)
    wg, wu, wd = _bf(w_exp_gate), _bf(w_exp_up), _bf(w_exp_down)

    sh1, sc1, g1, sh2, sc2, g2 = [mods(ada[0], i) for i in range(6)]
    b_g = row(b_gate_a[0])
    qp, kp, vp, rp, gp = _proj_a(xp, sh1[0], sc1[0], tpb, TM, w_main, w_gd, w_gu, b_g)
    qs, ks, vs, rs, gs = _proj_a(xs, sh1[1], sc1[1], 1, n_smp, w_main, w_gd, w_gu, b_g)
    gn = row(gn_a[0])
    ap, s_prompt = _gla_chunk(qp, kp, vp, rp, gp, gn, n_seq, t)
    as_, s_sample = _gla_step(qs, ks, vs, rs, gs, gn, state_gla[0])

    w_oa = _bf(w_o_a[0])
    lng, lnb = row(ln_mix_g[0]), row(ln_mix_b[0])
    x1p, rows_p, wts_p, meta_p, cnt_p = _post(ap, w_oa, xp, g1[0], lng, lnb, sh2[0], sc2[0], tpb, TM,
                                              wr_split, br_col)
    x1s, rows_s, wts_s, meta_s, cnt_s = _post(as_, w_oa, xs, g1[1], lng, lnb, sh2[1], sc2[1], 1, n_smp,
                                              wr_split, br_col)
    ffp, ffs = _moe(rows_p, meta_p, cnt_p, rows_s, meta_s, cnt_s, wg, wu, wd, 0)

    shkv, sckv = mods(ada_kv, 0), mods(ada_kv, 1)
    sh1, sc1, g1b, sh2b, sc2b, g2b = [mods(ada[1], i) for i in range(6)]
    lng, lnb = row(ln_ffn_g[0]), row(ln_ffn_b[0])
    w_kvb, w_qb = _bf(w_kv), _bf(w_q_b[0])
    x2p, k_p, v_p, kb_p, vb_p, q_p = _mid(x1p, ffp, wts_p, g2[0], lng, lnb, shkv[0], sckv[0], sh1[0], sc1[0],
                                          tpb, TM, w_kvb, w_qb)
    x2s, k_s, v_s, _, _, q_s = _mid(x1s, ffs, wts_s, g2[1], lng, lnb, shkv[1], sckv[1], sh1[1], sc1[1],
                                    1, n_smp, w_kvb, w_qb)

    lam_init = 0.8 - 0.6 * math.exp(-0.3 * 1)
    slopes = jnp.asarray([2.0 ** (-8.0 * (i + 1) / H_B) for i in range(H_B)], f32)
    nslope_rows = -jnp.repeat(slopes, 2).reshape(2 * H_B, 1)
    lam_args = (row(lam_q1[0]), row(lam_k1[0]), row(lam_q2[0]), row(lam_k2[0]), row(subln_b[0]))
    bp = _dattn(slopes, q_p, kb_p, vb_p, *lam_args, n_seq, t, lam_init)
    bs = _dattn_step(page_table, q_s, k_s, v_s, cache_k, cache_v, nslope_rows, *lam_args, past_len, lam_init)

    w_ob = _bf(w_o_b[0])
    lng, lnb = row(ln_mix_g[1]), row(ln_mix_b[1])
    x3p, rows_p, wts_p, meta_p, cnt_p = _post(bp, w_ob, x2p, g1b[0], lng, lnb, sh2b[0], sc2b[0], tpb, TM,
                                              wr_split, br_col)
    x3s, rows_s, wts_s, meta_s, cnt_s = _post(bs, w_ob, x2s, g1b[1], lng, lnb, sh2b[1], sc2b[1], 1, n_smp,
                                              wr_split, br_col)
    ffp, ffs = _moe(rows_p, meta_p, cnt_p, rows_s, meta_s, cnt_s, wg, wu, wd, 1)
    lng, lnb = row(ln_ffn_g[1]), row(ln_ffn_b[1])
    y_p = _final(x3p, ffp, wts_p, g2b[0], lng, lnb, tpb, TM)
    y_s = _final(x3s, ffs, wts_s, g2b[1], lng, lnb, 1, n_smp)

    n_pg_p = t // PAGE
    return (y_p.reshape(n_seq, t, D), y_s.reshape(n_smp, 1, D),
            s_prompt[None], s_sample[None],
            k_p.reshape(n_seq, n_pg_p, PAGE, H_B, 2 * DH_B), v_p.reshape(n_seq, n_pg_p, PAGE, H_B, DV_B),
            k_s.reshape(n_smp, 1, H_B, 2 * DH_B), v_s.reshape(n_smp, 1, H_B, DV_B))
```

```python
import functools
import math

import jax
import jax.numpy as jnp
from jax import lax
from jax.experimental import pallas as pl
from jax.experimental.pallas import tpu as pltpu

f32 = jnp.float32
bf16 = jnp.bfloat16

D = 1024
H_A, DK_A, DV_A = 4, 128, 256
QK_A, VW_A = H_A * DK_A, H_A * DV_A
GATE_RANK = 16
GATE_TAU = 16.0
H_B, DH_B, DV_B = 8, 64, 128
N_EXPERTS, N_GROUPS, GROUP_SIZE = 16, 4, 4
D_FF = 512
N_CLASSES = N_GROUPS * 6
DEPTH = 2
ALPHA = (2.0 * DEPTH) ** 0.25
LN_EPS = 1e-5
RMS_EPS = 1e-6
NEG_BIG = -1e30
PAGE = 128

LANES = 128
VMEM_LIMIT = 48 * 1024 * 1024
TM = 256
TM_E = 256
TM_PERM = 1024
GLA_C = 64
GLA_SEQS = 4
TQ = 512
TH = TQ // 2
PAGES_PER_STEP = 8
SUB = 8
assert D == SUB * LANES


def _load_tokens(ref, n, rows_per_token):
    return jnp.concatenate([ref[pl.ds(h, n, stride=rows_per_token), :] for h in range(SUB)], axis=1)


def _store_tokens(ref, val, offset, rows_per_token):
    n = val.shape[0]
    for h in range(SUB):
        ref[pl.ds(offset + h, n, stride=rows_per_token), :] = val[:, h * LANES:(h + 1) * LANES]


def _bf(a):
    return a.astype(bf16)


def _mm(a, b):
    return jnp.dot(a, b, preferred_element_type=f32)


def _mm_nt(a, b):
    return lax.dot_general(a, b, (((1,), (1,)), ((), ())), preferred_element_type=f32)


def _mm_tn(a, b):
    return lax.dot_general(a, b, (((0,), (0,)), ((), ())), preferred_element_type=f32)


def _params(*sem):
    return pltpu.CompilerParams(dimension_semantics=sem, vmem_limit_bytes=VMEM_LIMIT)


def _layer_norm(h, g, b):
    mu = jnp.mean(h, axis=-1, keepdims=True)
    hc = h - mu
    var = jnp.mean(hc * hc, axis=-1, keepdims=True)
    return hc * lax.rsqrt(var + LN_EPS) * g + b


def _rms_norm(o, g):
    ms = jnp.mean(o * o, axis=-1, keepdims=True)
    return o * lax.rsqrt(ms + RMS_EPS) * g


def _silu(x):
    return x / (1.0 + jnp.exp(-x))


def _row_to_col(row):
    n = row.shape[1]
    eye = lax.broadcasted_iota(jnp.int32, (n, n), 0) == lax.broadcasted_iota(jnp.int32, (n, n), 1)
    return jnp.sum(jnp.where(eye, row, 0.0), axis=1, keepdims=True)


def _ada_kernel(c_ref, w_ref, b_ref, o_ref):
    o_ref[...] = _mm(_bf(c_ref[...]), _bf(w_ref[...])) + b_ref[...]


def _ada(c_all, w3, b3, tn=1024):
    n_l, _, n = w3.shape
    m = c_all.shape[0]
    return pl.pallas_call(
        _ada_kernel,
        grid=(n_l, n // tn),
        in_specs=[pl.BlockSpec((m, D), lambda l, j: (0, 0)),
                  pl.BlockSpec((None, D, tn), lambda l, j: (l, 0, j)),
                  pl.BlockSpec((None, 1, tn), lambda l, j: (l, 0, j))],
        out_specs=pl.BlockSpec((None, m, tn), lambda l, j: (l, 0, j)),
        out_shape=jax.ShapeDtypeStruct((n_l, m, n), f32),
        compiler_params=_params("parallel", "parallel"),
        name="ada",
    )(c_all, w3, b3)


def _mod_spec(mod, tiles_per_block):
    return pl.BlockSpec((None, mod.shape[1], D), lambda i, *_: (i // tiles_per_block, 0, 0))


def _row_spec(tm, width):
    return pl.BlockSpec((tm, width), lambda i, *_: (i, 0))


def _const_spec(shape):
    nd = len(shape)
    return pl.BlockSpec(shape, lambda i, *_: (0,) * nd)


def _proj_a_kernel(x_ref, sh_ref, sc_ref, w_ref, wgd_ref, wgu_ref, bg_ref,
                   q_ref, k_ref, v_ref, r_ref, g_ref):
    xm = _bf(x_ref[...] * (1.0 + sc_ref[...]) + sh_ref[...])
    q_ref[...] = _mm(xm, w_ref[:, 0:QK_A]) * (DK_A ** -0.5)
    k_ref[...] = _mm(xm, w_ref[:, QK_A:2 * QK_A])
    v_ref[...] = _mm(xm, w_ref[:, 2 * QK_A:2 * QK_A + VW_A])
    r_ref[...] = _mm(xm, w_ref[:, 2 * QK_A + VW_A:2 * QK_A + 2 * VW_A])
    gd = _mm(xm, wgd_ref[...])
    z = _mm(_bf(gd), wgu_ref[...]) + bg_ref[...]
    g_ref[...] = (jnp.minimum(z, 0.0) - jnp.log1p(jnp.exp(-jnp.abs(z)))) * (1.0 / GATE_TAU)


def _proj_a(x, sh, sc, tiles_per_block, tm, w_main, w_gd, w_gu, b_g):
    n = x.shape[0]
    outs = [jax.ShapeDtypeStruct((n, QK_A), f32), jax.ShapeDtypeStruct((n, QK_A), f32),
            jax.ShapeDtypeStruct((n, VW_A), f32), jax.ShapeDtypeStruct((n, VW_A), f32),
            jax.ShapeDtypeStruct((n, QK_A), f32)]
    return pl.pallas_call(
        _proj_a_kernel,
        grid=(n // tm,),
        in_specs=[_row_spec(tm, D), _mod_spec(sh, tiles_per_block), _mod_spec(sc, tiles_per_block),
                  _const_spec(w_main.shape), _const_spec(w_gd.shape), _const_spec(w_gu.shape),
                  _const_spec(b_g.shape)],
        out_specs=[_row_spec(tm, QK_A), _row_spec(tm, QK_A), _row_spec(tm, VW_A),
                   _row_spec(tm, VW_A), _row_spec(tm, QK_A)],
        out_shape=outs,
        compiler_params=_params("parallel"),
        name="proj_a",
    )(x, sh, sc, w_main, w_gd, w_gu, b_g)


def _gla_chunk_kernel(q_ref, k_ref, v_ref, r_ref, g_ref, gn_ref, o_ref, s_ref, st_scr, *, n_chunks):
    c = pl.program_id(1)
    C = q_ref.shape[1]

    @pl.when(c == 0)
    def _():
        st_scr[...] = jnp.zeros_like(st_scr)

    row = lax.broadcasted_iota(jnp.int32, (C, C), 0)
    col = lax.broadcasted_iota(jnp.int32, (C, C), 1)
    tril = col <= row
    tril_b = jnp.where(tril, 1.0, 0.0).astype(bf16)
    mid = C // 2 - 1
    for b in range(q_ref.shape[0]):
        g = g_ref[b]
        g1 = _bf(g)
        rem = g - g1.astype(f32)
        g2 = _bf(rem)
        g3 = _bf(rem - g2.astype(f32))
        cum = _mm(tril_b, g1) + _mm(tril_b, g2) + _mm(tril_b, g3)
        for h in range(H_A):
            ks = slice(h * DK_A, (h + 1) * DK_A)
            vs = slice(h * DV_A, (h + 1) * DV_A)
            cum_h = cum[:, ks]
            m = cum_h[mid:mid + 1, :]
            last = cum_h[C - 1:C, :]
            qh = q_ref[b, :, ks]
            kh = k_ref[b, :, ks]
            vb = _bf(v_ref[b, :, vs])
            q_in = _bf(qh * jnp.exp(cum_h))
            q_mid = _bf(qh * jnp.exp(cum_h - m))
            k_mid = _bf(kh * jnp.exp(m - cum_h))
            k_last = _bf(kh * jnp.exp(last - cum_h))
            st = st_scr[b, h]
            inter = _mm_nt(q_in, _bf(st))
            attn = jnp.where(tril, _mm_nt(q_mid, k_mid), 0.0)
            o = inter + _mm(_bf(attn), vb)
            st_new = st * jnp.exp(last) + _mm_tn(vb, k_last)
            st_scr[b, h] = st_new
            rh = r_ref[b, :, vs]
            o_ref[b, :, vs] = _bf(_rms_norm(o, gn_ref[...]) * _silu(rh))

    @pl.when(c == n_chunks - 1)
    def _():
        for b in range(q_ref.shape[0]):
            for h in range(H_A):
                s_ref[b, h] = st_scr[b, h].T


def _gla_chunk(q, k, v, r, g, gn, n_seq, t):
    n_chunks = t // GLA_C
    nb = GLA_SEQS
    rows = lambda w: pl.BlockSpec((nb, GLA_C, w), lambda b, c: (b, c, 0))
    seq = lambda a: a.reshape(n_seq, t, a.shape[1])
    o, s_fin = pl.pallas_call(
        functools.partial(_gla_chunk_kernel, n_chunks=n_chunks),
        grid=(n_seq // nb, n_chunks),
        in_specs=[rows(QK_A), rows(QK_A), rows(VW_A), rows(VW_A), rows(QK_A),
                  pl.BlockSpec((1, DV_A), lambda b, c: (0, 0))],
        out_specs=[rows(VW_A), pl.BlockSpec((nb, H_A, DK_A, DV_A), lambda b, c: (b, 0, 0, 0))],
        out_shape=[jax.ShapeDtypeStruct((n_seq, t, VW_A), bf16),
                   jax.ShapeDtypeStruct((n_seq, H_A, DK_A, DV_A), f32)],
        scratch_shapes=[pltpu.VMEM((nb, H_A, DV_A, DK_A), f32)],
        compiler_params=_params("parallel", "arbitrary"),
        name="gla_chunk",
    )(seq(q), seq(k), seq(v), seq(r), seq(g), gn)
    return o.reshape(n_seq * t, VW_A), s_fin


def _gla_step_kernel(q_ref, k_ref, v_ref, r_ref, g_ref, gn_ref, s_ref, o_ref, sn_ref):
    for h in range(H_A):
        ks = slice(h * DK_A, (h + 1) * DK_A)
        vs = slice(h * DV_A, (h + 1) * DV_A)
        q_col = _row_to_col(q_ref[:, ks])
        k_col = _row_to_col(k_ref[:, ks])
        eg_col = _row_to_col(jnp.exp(g_ref[:, ks]))
        s_new = eg_col * s_ref[h] + k_col * v_ref[:, vs]
        sn_ref[h] = s_new
        o = jnp.sum(q_col * s_new, axis=0, keepdims=True)
        rh = r_ref[:, vs]
        o_ref[:, vs] = _rms_norm(o, gn_ref[...]) * _silu(rh)


def _gla_step(q, k, v, r, g, gn, state):
    n = q.shape[0]
    vec = lambda w: pl.BlockSpec((None, 1, w), lambda b: (b, 0, 0))
    st = pl.BlockSpec((None, H_A, DK_A, DV_A), lambda b: (b, 0, 0, 0))
    o, s_new = pl.pallas_call(
        _gla_step_kernel,
        grid=(n,),
        in_specs=[vec(QK_A), vec(QK_A), vec(VW_A), vec(VW_A), vec(QK_A),
                  pl.BlockSpec((1, DV_A), lambda b: (0, 0)), st],
        out_specs=[vec(VW_A), st],
        out_shape=[jax.ShapeDtypeStruct((n, 1, VW_A), f32),
                   jax.ShapeDtypeStruct((n, H_A, DK_A, DV_A), f32)],
        compiler_params=_params("parallel"),
        name="gla_step",
    )(q.reshape(n, 1, QK_A), k.reshape(n, 1, QK_A), v.reshape(n, 1, VW_A), r.reshape(n, 1, VW_A),
      g.reshape(n, 1, QK_A), gn, state)
    return o.reshape(n, VW_A), s_new


def _route(probs):
    p = [probs[e:e + 1, :] for e in range(N_EXPERTS)]
    scores = []
    for gi in range(N_GROUPS):
        a0, a1, a2, a3 = p[4 * gi:4 * gi + 4]
        hi1, lo1 = jnp.maximum(a0, a1), jnp.minimum(a0, a1)
        hi2, lo2 = jnp.maximum(a2, a3), jnp.minimum(a2, a3)
        scores.append(jnp.maximum(hi1, hi2) + jnp.maximum(jnp.minimum(hi1, hi2), jnp.maximum(lo1, lo2)))
    best = scores[0]
    gidx = jnp.zeros(best.shape, jnp.int32)
    for gi in range(1, N_GROUPS):
        upd = scores[gi] > best
        best = jnp.where(upd, scores[gi], best)
        gidx = jnp.where(upd, gi, gidx)
    a = [jnp.where(gidx == 0, p[j], jnp.where(gidx == 1, p[4 + j], jnp.where(gidx == 2, p[8 + j], p[12 + j])))
         for j in range(GROUP_SIZE)]
    v1 = a[0]
    i1 = jnp.zeros(best.shape, jnp.int32)
    for j in range(1, GROUP_SIZE):
        upd = a[j] > v1
        v1 = jnp.where(upd, a[j], v1)
        i1 = jnp.where(upd, j, i1)
    v2 = jnp.full(best.shape, -1.0, f32)
    i2 = jnp.zeros(best.shape, jnp.int32)
    for j in range(GROUP_SIZE):
        upd = jnp.logical_and(i1 != j, a[j] > v2)
        v2 = jnp.where(upd, a[j], v2)
        i2 = jnp.where(upd, j, i2)
    tot = v1 + v2
    w1 = v1 / tot
    w2 = v2 / tot
    first_low = i1 < i2
    lo = jnp.minimum(i1, i2)
    hi = jnp.maximum(i1, i2)
    pair = lo * 3 - ((lo * (lo - 1)) >> 1) + hi - lo - 1
    cls = gidx * 6 + pair
    return cls, jnp.where(first_low, w1, w2), jnp.where(first_low, w2, w1)


def _post_kernel(a_ref, w_ref, x_ref, g1_ref, lng_ref, lnb_ref, sh2_ref, sc2_ref, wr_ref, br_ref,
                 x1_ref, rows_ref, wts_ref, meta_ref, cnt_ref, cnt_scr):
    i = pl.program_id(0)
    tm = x_ref.shape[0]

    @pl.when(i == 0)
    def _():
        cnt_scr[...] = jnp.zeros_like(cnt_scr)

    mix = _mm(_bf(a_ref[...]), w_ref[...])
    x1 = _layer_norm(ALPHA * x_ref[...] + g1_ref[...] * mix, lng_ref[...], lnb_ref[...])
    x1_ref[...] = x1
    xm = x1 * (1.0 + sc2_ref[...]) + sh2_ref[...]
    _store_tokens(rows_ref, xm, 0, SUB)

    x_hi = _bf(xm)
    x_lo = _bf(xm - x_hi.astype(f32))
    wr = wr_ref[...]
    l_hi = _mm_nt(wr, x_hi)
    l_lo = _mm_nt(wr[0:N_EXPERTS], x_lo)
    logits = l_hi[0:N_EXPERTS] + l_hi[N_EXPERTS:2 * N_EXPERTS] + l_lo + br_ref[...]
    mx = jnp.max(logits, axis=0, keepdims=True)
    ex = jnp.exp(logits - mx)
    probs = ex / jnp.sum(ex, axis=0, keepdims=True)
    cls, w_lo, w_hi = _route(probs)

    crow = lax.broadcasted_iota(jnp.int32, (32, tm), 0)
    onehot = crow == cls
    oh = jnp.where(onehot, 1.0, 0.0)
    before = lax.broadcasted_iota(jnp.int32, (tm, tm), 0) < lax.broadcasted_iota(jnp.int32, (tm, tm), 1)
    prefix = _mm(_bf(oh), jnp.where(before, 1.0, 0.0).astype(bf16))
    carry = cnt_scr[...]
    rank = jnp.sum(jnp.where(onehot, prefix + carry, 0.0), axis=0, keepdims=True)
    cnt_new = carry + jnp.sum(oh, axis=1, keepdims=True)
    cnt_scr[...] = cnt_new
    cnt_ref[...] = jnp.broadcast_to(cnt_new, cnt_ref.shape)

    mrow = lax.broadcasted_iota(jnp.int32, (8, tm), 0)
    meta_ref[...] = jnp.where(mrow == 0, cls, jnp.where(mrow == 1, rank.astype(jnp.int32), 0))

    lane = lax.broadcasted_iota(jnp.int32, (tm, LANES), 1)
    wts_ref[...] = jnp.where(lane == 0, _row_to_col(w_lo), jnp.where(lane == 1, _row_to_col(w_hi), 0.0))


def _post(a, w_o, x, g1, lng, lnb, sh2, sc2, tiles_per_block, tm, wr_t, br_col):
    n = x.shape[0]
    mod = lambda arr: _mod_spec(arr, tiles_per_block)
    return pl.pallas_call(
        _post_kernel,
        grid=(n // tm,),
        in_specs=[_row_spec(tm, D), _const_spec(w_o.shape), _row_spec(tm, D), mod(g1),
                  _const_spec(lng.shape), _const_spec(lnb.shape), mod(sh2), mod(sc2),
                  _const_spec(wr_t.shape), _const_spec(br_col.shape)],
        out_specs=[_row_spec(tm, D), _row_spec(tm * SUB, LANES), _row_spec(tm, LANES),
                   pl.BlockSpec((8, tm), lambda i: (0, i)),
                   pl.BlockSpec((32, LANES), lambda i: (0, 0))],
        out_shape=[jax.ShapeDtypeStruct((n, D), f32), jax.ShapeDtypeStruct((n * SUB, LANES), f32),
                   jax.ShapeDtypeStruct((n, LANES), f32), jax.ShapeDtypeStruct((8, n), jnp.int32),
                   jax.ShapeDtypeStruct((32, LANES), f32)],
        scratch_shapes=[pltpu.VMEM((32, 1), f32)],
        compiler_params=_params("arbitrary"),
        name="post",
    )(a, w_o, x, g1, lng, lnb, sh2, sc2, wr_t, br_col)


def _scatter_kernel(pos_ref, src_ref, dst_in_ref, dst_ref, sem, *, tm, rows):
    del dst_in_ref
    base = pl.program_id(0) * tm

    def token_copy(r):
        slot = pl.multiple_of(pos_ref[base + r] * rows, rows)
        return pltpu.make_async_copy(src_ref.at[pl.ds(pl.multiple_of(r * rows, rows), rows)],
                                     dst_ref.at[pl.ds(slot, rows)], sem)

    def start(r, carry):
        token_copy(r).start()
        return carry

    def wait(r, carry):
        token_copy(r).wait()
        return carry

    lax.fori_loop(0, tm, start, 0, unroll=8)
    lax.fori_loop(0, tm, wait, 0, unroll=8)


def _scatter_rows(pos, src, dst, rows):
    n = pos.shape[0]
    tm = min(TM_PERM, n)
    return pl.pallas_call(
        functools.partial(_scatter_kernel, tm=tm, rows=rows),
        grid_spec=pltpu.PrefetchScalarGridSpec(
            num_scalar_prefetch=1, grid=(n // tm,),
            in_specs=[pl.BlockSpec((tm * rows, LANES), lambda t, pos: (t, 0)),
                      pl.BlockSpec(memory_space=pl.ANY)],
            out_specs=pl.BlockSpec(memory_space=pl.ANY),
            scratch_shapes=[pltpu.SemaphoreType.DMA(())]),
        out_shape=jax.ShapeDtypeStruct(dst.shape, dst.dtype),
        input_output_aliases={2: 0},
        compiler_params=_params("arbitrary"),
        name="scatter_rows",
    )(pos, src, dst)


def _gather_tokens(pos_ref, src_ref, buf_ref, sem, tile, tm, rows, wait):
    base = tile * tm

    def token_copy(r):
        slot = pl.multiple_of(pos_ref[base + r] * rows, rows)
        return pltpu.make_async_copy(src_ref.at[pl.ds(slot, rows)],
                                     buf_ref.at[pl.ds(pl.multiple_of(r * rows, rows), rows)], sem)

    def step(r, carry):
        if wait:
            token_copy(r).wait()
        else:
            token_copy(r).start()
        return carry

    lax.fori_loop(0, tm, step, 0, unroll=8)


def _prefetched_tokens(pos_ref, src_ref, buf_ref, sem_ref, tm, rows):
    i = pl.program_id(0)
    slot = i % 2

    @pl.when(i == 0)
    def _():
        _gather_tokens(pos_ref, src_ref, buf_ref.at[0], sem_ref.at[0], 0, tm, rows, wait=False)

    @pl.when(i + 1 < pl.num_programs(0))
    def _():
        _gather_tokens(pos_ref, src_ref, buf_ref.at[1 - slot], sem_ref.at[1 - slot], i + 1, tm, rows, wait=False)

    _gather_tokens(pos_ref, src_ref, buf_ref.at[slot], sem_ref.at[slot], i, tm, rows, wait=True)
    return buf_ref.at[slot]


def _experts_kernel(elo_ref, ehi_ref, valid_ref, rows_ref,
                    wg_lo, wu_lo, wd_lo, wg_hi, wu_hi, wd_hi, o_ref):
    t = pl.program_id(0)

    @pl.when(valid_ref[t] == 1)
    def _():
        xb = _bf(_load_tokens(rows_ref, TM_E, SUB))

        def ffn(wg, wu, wd):
            hid = _silu(_mm(xb, wg[...])) * _mm(xb, wu[...])
            return _mm(_bf(hid), wd[...])

        _store_tokens(o_ref, ffn(wg_lo, wu_lo, wd_lo), 0, 2 * SUB)
        _store_tokens(o_ref, ffn(wg_hi, wu_hi, wd_hi), SUB, 2 * SUB)

    @pl.when(valid_ref[t] == 0)
    def _():
        o_ref[...] = jnp.zeros_like(o_ref)


def _experts(e_lo, e_hi, valid, rows, wg, wu, wd, layer):
    n_tiles = rows.shape[0] // (TM_E * SUB)
    up = lambda sel: pl.BlockSpec((None, None, D, D_FF), lambda t, lo, hi, v: (layer, sel(lo, hi)[t], 0, 0))
    down = lambda sel: pl.BlockSpec((None, None, D_FF, D), lambda t, lo, hi, v: (layer, sel(lo, hi)[t], 0, 0))
    first = lambda lo, hi: lo
    second = lambda lo, hi: hi
    return pl.pallas_call(
        _experts_kernel,
        grid_spec=pltpu.PrefetchScalarGridSpec(
            num_scalar_prefetch=3, grid=(n_tiles,),
            in_specs=[pl.BlockSpec((TM_E * SUB, LANES), lambda t, lo, hi, v: (t, 0)),
                      up(first), up(first), down(first), up(second), up(second), down(second)],
            out_specs=pl.BlockSpec((TM_E * 2 * SUB, LANES), lambda t, lo, hi, v: (t, 0))),
        out_shape=jax.ShapeDtypeStruct((2 * rows.shape[0], LANES), f32),
        compiler_params=_params("parallel"),
        name="experts",
    )(e_lo, e_hi, valid, rows, wg, wu, wd, wg, wu, wd)


def _moe(rows_p, meta_p, cnt_p, rows_s, meta_s, cnt_s, wg, wu, wd, layer):
    n_p, n_s = meta_p.shape[1], meta_s.shape[1]
    n_tiles = -(-(n_p + n_s) // TM_E) + N_CLASSES
    cp = cnt_p[:N_CLASSES, 0].astype(jnp.int32)
    cs = cnt_s[:N_CLASSES, 0].astype(jnp.int32)
    tiles_c = (cp + cs + TM_E - 1) // TM_E
    tile_end = jnp.cumsum(tiles_c)
    row_off = (tile_end - tiles_c) * TM_E
    pos_p = row_off[meta_p[0]] + meta_p[1]
    pos_s = row_off[meta_s[0]] + cp[meta_s[0]] + meta_s[1]
    t_idx = jnp.arange(n_tiles, dtype=jnp.int32)
    valid = (t_idx < tile_end[-1]).astype(jnp.int32)
    tile_cls = jnp.minimum(jnp.sum((tile_end[None, :] <= t_idx[:, None]).astype(jnp.int32), axis=1), N_CLASSES - 1)
    pairs = [(a, b) for a in range(GROUP_SIZE) for b in range(a + 1, GROUP_SIZE)]
    lo_tab = jnp.asarray([gi * GROUP_SIZE + a for gi in range(N_GROUPS) for a, _ in pairs], jnp.int32)
    hi_tab = jnp.asarray([gi * GROUP_SIZE + b for gi in range(N_GROUPS) for _, b in pairs], jnp.int32)
    e_lo, e_hi = lo_tab[tile_cls], hi_tab[tile_cls]

    buf = jnp.zeros((n_tiles * TM_E * SUB, LANES), f32)
    buf = _scatter_rows(pos_p, rows_p, buf, SUB)
    buf = _scatter_rows(pos_s, rows_s, buf, SUB)
    ff_sorted = _experts(e_lo, e_hi, valid, buf, wg, wu, wd, layer)
    return ff_sorted, pos_p, pos_s


def _ffn_out(pos_ref, sorted_ref, wts_ref, buf_ref, sem_ref, n):
    pair_ref = _prefetched_tokens(pos_ref, sorted_ref, buf_ref, sem_ref, n, 2 * SUB)
    wts = wts_ref[...]
    lo = jnp.concatenate([pair_ref[pl.ds(h, n, stride=2 * SUB), :] for h in range(SUB)], axis=1)
    hi = jnp.concatenate([pair_ref[pl.ds(SUB + h, n, stride=2 * SUB), :] for h in range(SUB)], axis=1)
    return wts[:, 0:1] * lo + wts[:, 1:2] * hi


def _mid_kernel(pos_ref, x1_ref, sorted_ref, wts_ref, g2_ref, lng_ref, lnb_ref, shkv_ref, sckv_ref, shq_ref,
                scq_ref, wkv_ref, wq_ref, x2_ref, k_ref, v_ref, kb_ref, vb_ref, q_ref, buf_ref, sem_ref):
    ff = _ffn_out(pos_ref, sorted_ref, wts_ref, buf_ref, sem_ref, x1_ref.shape[0])
    x2 = _layer_norm(ALPHA * x1_ref[...] + g2_ref[...] * ff, lng_ref[...], lnb_ref[...])
    x2_ref[...] = x2
    xkv = _bf(x2 * (1.0 + sckv_ref[...]) + shkv_ref[...])
    kk = _mm(xkv, wkv_ref[:, 0:D])
    vv = _mm(xkv, wkv_ref[:, D:2 * D])
    _store_tokens(k_ref, kk, 0, SUB)
    _store_tokens(v_ref, vv, 0, SUB)
    kb_ref[...] = _bf(kk)
    vb_ref[...] = _bf(vv)
    xq = _bf(x2 * (1.0 + scq_ref[...]) + shq_ref[...])
    q_ref[...] = _bf(_mm(xq, wq_ref[...]) * (DH_B ** -0.5))


def _pair_scratch(tm):
    return [pltpu.VMEM((2, tm * 2 * SUB, LANES), f32), pltpu.SemaphoreType.DMA((2,))]


def _mid(pos, x1, ff_sorted, wts, g2, lng, lnb, shkv, sckv, shq, scq, tiles_per_block, tm, w_kv, w_q):
    n = x1.shape[0]
    mod = lambda arr: _mod_spec(arr, tiles_per_block)
    sds = lambda dt: jax.ShapeDtypeStruct((n, D), dt)
    kv_sds = jax.ShapeDtypeStruct((n * SUB, LANES), f32)
    return pl.pallas_call(
        _mid_kernel,
        grid_spec=pltpu.PrefetchScalarGridSpec(
            num_scalar_prefetch=1, grid=(n // tm,),
            in_specs=[_row_spec(tm, D), pl.BlockSpec(memory_space=pl.ANY), _row_spec(tm, LANES), mod(g2),
                      _const_spec(lng.shape), _const_spec(lnb.shape),
                      mod(shkv), mod(sckv), mod(shq), mod(scq), _const_spec(w_kv.shape), _const_spec(w_q.shape)],
            out_specs=[_row_spec(tm, D), _row_spec(tm * SUB, LANES), _row_spec(tm * SUB, LANES),
                       _row_spec(tm, D), _row_spec(tm, D), _row_spec(tm, D)],
            scratch_shapes=_pair_scratch(tm)),
        out_shape=[sds(f32), kv_sds, kv_sds, sds(bf16), sds(bf16), sds(bf16)],
        compiler_params=_params("arbitrary"),
        name="mid",
    )(pos, x1, ff_sorted, wts, g2, lng, lnb, shkv, sckv, shq, scq, w_kv, w_q)


def _lambda(lq1_ref, lk1_ref, lq2_ref, lk2_ref, lam_init):
    s1 = jnp.sum(lq1_ref[...] * lk1_ref[...], axis=1, keepdims=True)
    s2 = jnp.sum(lq2_ref[...] * lk2_ref[...], axis=1, keepdims=True)
    return jnp.exp(s1) - jnp.exp(s2) + lam_init


def _online_update(s, v, m_ref, l_ref, acc_ref):
    m_old = m_ref[...]
    m_new = jnp.maximum(m_old, jnp.max(s, axis=1, keepdims=True))
    p = jnp.exp(s - jnp.tile(m_new, (1, s.shape[1] // LANES)))
    corr = jnp.exp(m_old - m_new)
    l_ref[...] = l_ref[...] * corr + jnp.sum(p, axis=1, keepdims=True)
    acc_ref[...] = acc_ref[...] * jnp.tile(corr, (1, acc_ref.shape[1] // LANES)) + _mm(p.astype(v.dtype), v)
    m_ref[...] = m_new


def _dattn_kernel(slope_ref, q_ref, k_ref, v_ref, lq1_ref, lk1_ref, lq2_ref, lk2_ref, subln_ref,
                  o_ref, m_scr, l_scr, acc_scr, *, lam_init):
    t = q_ref.shape[0]
    slope = slope_ref[pl.program_id(1)]
    lam = _lambda(lq1_ref, lk1_ref, lq2_ref, lk2_ref, lam_init)
    first_map = lax.broadcasted_iota(jnp.int32, (1, 2 * DH_B), 1) < DH_B
    rel_q = lax.broadcasted_iota(jnp.int32, (TH, 1), 0)

    def attend(chain, qm, q0, k0, nk, masked):
        s = _mm_nt(qm, k_ref[pl.ds(k0, nk), :])
        rel_k = lax.broadcasted_iota(jnp.int32, (1, nk), 1) + (k0 - q0)
        s = s + slope * rel_k.astype(f32)
        if masked:
            s = jnp.where(rel_k <= rel_q, s, NEG_BIG)
        _online_update(s, v_ref[pl.ds(k0, nk), :], m_scr.at[chain], l_scr.at[chain], acc_scr.at[chain])

    for i in range(t // TQ):
        q0 = i * TQ
        m_scr[...] = jnp.full(m_scr.shape, NEG_BIG, f32)
        l_scr[...] = jnp.zeros_like(l_scr)
        acc_scr[...] = jnp.zeros_like(acc_scr)
        chains = []
        for half in range(2):
            q = q_ref[q0 + half * TH:q0 + (half + 1) * TH, :]
            zero = jnp.zeros_like(q)
            chains.append((2 * half, jnp.where(first_map, q, zero), q0 + half * TH))
            chains.append((2 * half + 1, jnp.where(first_map, zero, q), q0 + half * TH))

        def below_diagonal(j, carry):
            k0 = pl.multiple_of(j * TQ, TQ)
            for chain, qm, row0 in chains:
                attend(chain, qm, row0, k0, TQ, False)
            return carry

        lax.fori_loop(0, i, below_diagonal, 0)
        for chain, qm, row0 in chains:
            if row0 > q0:
                attend(chain, qm, row0, q0, TH, False)
            attend(chain, qm, row0, row0, TH, True)

        for half in range(2):
            o1 = acc_scr[2 * half] / l_scr[2 * half]
            o2 = acc_scr[2 * half + 1] / l_scr[2 * half + 1]
            o = o1 - lam * o2
            o_ref[q0 + half * TH:q0 + (half + 1) * TH, :] = _bf(_rms_norm(o, subln_ref[...]) * (1.0 - lam_init))


def _dattn(slopes, q, k, v, lq1, lk1, lq2, lk2, subln, n_seq, t, lam_init):
    head = pl.BlockSpec((t, 2 * DH_B), lambda b, h: (b, h))
    small = lambda w: pl.BlockSpec((1, w), lambda b, h: (0, 0))
    return pl.pallas_call(
        functools.partial(_dattn_kernel, lam_init=lam_init),
        grid=(n_seq, H_B),
        in_specs=[pl.BlockSpec(memory_space=pltpu.SMEM), head, head, head,
                  small(DH_B), small(DH_B), small(DH_B), small(DH_B), small(DV_B)],
        out_specs=head,
        out_shape=jax.ShapeDtypeStruct((n_seq * t, H_B * DV_B), bf16),
        scratch_shapes=[pltpu.VMEM((4, TH, LANES), f32), pltpu.VMEM((4, TH, LANES), f32),
                        pltpu.VMEM((4, TH, DV_B), f32)],
        compiler_params=_params("parallel", "parallel"),
        name="dattn_prompt",
    )(slopes, q, k, v, lq1, lk1, lq2, lk2, subln)


def _dattn_step_kernel(pt_ref, q_ref, kn_ref, vn_ref, nslope_ref, lq1_ref, lk1_ref, lq2_ref, lk2_ref,
                       subln_ref, *rest, past_len, lam_init):
    n_pg = PAGES_PER_STEP
    k_refs, v_refs = rest[0:n_pg], rest[n_pg:2 * n_pg]
    o_ref, qbd_scr, m_scr, l_scr, acc_scr = rest[2 * n_pg:]
    j = pl.program_id(1)
    rows = 2 * H_B

    @pl.when(j == 0)
    def _():
        lane = lax.broadcasted_iota(jnp.int32, (rows, D), 1)
        rowi = lax.broadcasted_iota(jnp.int32, (rows, D), 0)
        q = jnp.broadcast_to(q_ref[...].astype(f32), (rows, D))
        qbd_scr[...] = jnp.where((lane >> 6) == rowi, q, 0.0)
        m_scr[...] = jnp.full(m_scr.shape, NEG_BIG, f32)
        l_scr[...] = jnp.zeros_like(l_scr)
        acc_scr[...] = jnp.zeros_like(acc_scr)

    qbd = qbd_scr[...]
    s = jnp.concatenate([_mm_nt(qbd, _load_tokens(k_refs[u], PAGE, SUB)) for u in range(n_pg)], axis=1)
    kpos = j * (n_pg * PAGE) + lax.broadcasted_iota(jnp.int32, (1, n_pg * PAGE), 1)
    s = s + nslope_ref[...] * (past_len - kpos).astype(f32)
    m_old = m_scr[...]
    m_new = jnp.maximum(m_old, jnp.max(s, axis=1, keepdims=True))
    p = jnp.exp(s - jnp.tile(m_new, (1, n_pg)))
    corr = jnp.exp(m_old - m_new)
    l_scr[...] = l_scr[...] * corr + jnp.sum(p, axis=1, keepdims=True)
    pv = _mm(p[:, 0:PAGE], _load_tokens(v_refs[0], PAGE, SUB))
    for u in range(1, n_pg):
        pv = pv + _mm(p[:, u * PAGE:(u + 1) * PAGE], _load_tokens(v_refs[u], PAGE, SUB))
    acc_scr[...] = acc_scr[...] * jnp.tile(corr, (1, SUB)) + pv
    m_scr[...] = m_new

    @pl.when(j == pl.num_programs(1) - 1)
    def _():
        kn = jnp.concatenate([kn_ref[h:h + 1, :] for h in range(H_B)], axis=1)
        vn = jnp.concatenate([vn_ref[h:h + 1, :] for h in range(H_B)], axis=1)
        kn_b = _bf(kn).astype(f32)
        vn_b = _bf(vn).astype(f32)
        s = jnp.sum(qbd * kn_b, axis=1, keepdims=True)
        m_old = m_scr[...]
        m_new = jnp.maximum(m_old, s)
        p = jnp.exp(s - m_new)
        corr = jnp.exp(m_old - m_new)
        l_fin = l_scr[...] * corr + p
        acc = acc_scr[...] * jnp.tile(corr, (1, SUB)) + _bf(jnp.tile(p, (1, SUB))).astype(f32) * vn_b
        lane = lax.broadcasted_iota(jnp.int32, (rows, D), 1)
        rowi = lax.broadcasted_iota(jnp.int32, (rows, D), 0)
        a = jnp.where((lane >> 7) == (rowi >> 1), acc / jnp.tile(l_fin, (1, SUB)), 0.0)
        a1 = jnp.sum(jnp.where((rowi & 1) == 0, a, 0.0), axis=0, keepdims=True)
        a2 = jnp.sum(jnp.where((rowi & 1) == 1, a, 0.0), axis=0, keepdims=True)
        o = a1 - _lambda(lq1_ref, lk1_ref, lq2_ref, lk2_ref, lam_init) * a2
        for h in range(H_B):
            hs = slice(h * DV_B, (h + 1) * DV_B)
            o_ref[:, hs] = _rms_norm(o[:, hs], subln_ref[...]) * (1.0 - lam_init)


def _dattn_step(page_table, q, k_new, v_new, cache_k, cache_v, nslope_rows, lq1, lk1, lq2, lk2, subln,
                past_len, lam_init):
    n, n_pages = page_table.shape
    n_pool = cache_k.shape[0]
    ck = cache_k.reshape(n_pool, PAGE * H_B, DV_B)
    cv = cache_v.reshape(n_pool, PAGE * H_B, DV_B)
    n_pg = PAGES_PER_STEP
    vec = pl.BlockSpec((None, 1, D), lambda b, j, pt: (b, 0, 0))
    new_kv = pl.BlockSpec((SUB, LANES), lambda b, j, pt: (b, 0))
    const = lambda shape: pl.BlockSpec(shape, lambda b, j, pt: (0,) * len(shape))

    def page(u):
        return pl.BlockSpec((None, PAGE * H_B, DV_B), lambda b, j, pt: (pt[b, j * n_pg + u], 0, 0))

    pages = [page(u) for u in range(n_pg)]
    out = pl.pallas_call(
        functools.partial(_dattn_step_kernel, past_len=past_len, lam_init=lam_init),
        grid_spec=pltpu.PrefetchScalarGridSpec(
            num_scalar_prefetch=1, grid=(n, n_pages // n_pg),
            in_specs=[vec, new_kv, new_kv, const((2 * H_B, 1)), const((1, DH_B)), const((1, DH_B)),
                      const((1, DH_B)), const((1, DH_B)), const((1, DV_B))] + pages + pages,
            out_specs=vec,
            scratch_shapes=[pltpu.VMEM((2 * H_B, D), f32), pltpu.VMEM((2 * H_B, LANES), f32),
                            pltpu.VMEM((2 * H_B, LANES), f32), pltpu.VMEM((2 * H_B, D), f32)]),
        out_shape=jax.ShapeDtypeStruct((n, 1, D), f32),
        compiler_params=_params("parallel", "arbitrary"),
        name="dattn_step",
    )(page_table, q.reshape(n, 1, D), k_new, v_new, nslope_rows,
      lq1, lk1, lq2, lk2, subln, *([ck] * n_pg), *([cv] * n_pg))
    return out.reshape(n, D)


def _final_kernel(pos_ref, x_ref, sorted_ref, wts_ref, g2_ref, lng_ref, lnb_ref, y_ref, buf_ref, sem_ref):
    ff = _ffn_out(pos_ref, sorted_ref, wts_ref, buf_ref, sem_ref, x_ref.shape[0])
    y_ref[...] = _layer_norm(ALPHA * x_ref[...] + g2_ref[...] * ff, lng_ref[...], lnb_ref[...])


def _final(pos, x, ff_sorted, wts, g2, lng, lnb, tiles_per_block, tm):
    n = x.shape[0]
    return pl.pallas_call(
        _final_kernel,
        grid_spec=pltpu.PrefetchScalarGridSpec(
            num_scalar_prefetch=1, grid=(n // tm,),
            in_specs=[_row_spec(tm, D), pl.BlockSpec(memory_space=pl.ANY), _row_spec(tm, LANES),
                      _mod_spec(g2, tiles_per_block), _const_spec(lng.shape), _const_spec(lnb.shape)],
            out_specs=_row_spec(tm, D),
            scratch_shapes=_pair_scratch(tm)),
        out_shape=jax.ShapeDtypeStruct((n, D), f32),
        compiler_params=_params("arbitrary"),
        name="final",
    )(pos, x, ff_sorted, wts, g2, lng, lnb)


def kernel(x_prompt, x_sample, c_prompt, c_sample, state_gla, cache_k, cache_v, page_table, w_ada, b_ada,
           ln_mix_g, ln_mix_b, ln_ffn_g, ln_ffn_b, w_in_a, w_gate_up_a, b_gate_a, gn_a, w_o_a, w_ada_kv,
           b_ada_kv, w_kv, w_q_b, lam_q1, lam_k1, lam_q2, lam_k2, subln_b, w_o_b, w_router, b_router,
           w_exp_gate, w_exp_up, w_exp_down):
    n_seq, t, _ = x_prompt.shape
    n_smp = x_sample.shape[0]
    n_p = n_seq * t
    past_len = page_table.shape[1] * PAGE
    xp = x_prompt.reshape(n_p, D)
    xs = x_sample.reshape(n_smp, D)
    tpb = t // TM

    c_all = jnp.concatenate([c_prompt, c_sample], axis=0)
    ada = _ada(c_all, w_ada, b_ada.reshape(DEPTH, 1, 6 * D))
    ada_kv = _ada(c_all, w_ada_kv.reshape(1, D, 2 * D), b_ada_kv.reshape(1, 1, 2 * D))[0]

    def mods(table, idx):
        part = table[:, idx * D:(idx + 1) * D]
        return part[:n_seq].reshape(n_seq, 1, D), part[n_seq:].reshape(1, n_smp, D)

    row = lambda a: a.reshape(1, -1)

    w_in = w_in_a[0]
    w_main = _bf(w_in[:, :2 * QK_A + 2 * VW_A])
    w_gd = _bf(jnp.pad(w_in[:, 2 * QK_A + 2 * VW_A:], ((0, 0), (0, LANES - GATE_RANK))))
    w_gu = _bf(jnp.pad(w_gate_up_a[0], ((0, LANES - GATE_RANK), (0, 0))))
    wr_t = w_router.T
    wr_hi = _bf(wr_t)
    wr_split = jnp.concatenate([wr_hi, _bf(wr_t - wr_hi.astype(f32))], axis=0)
    br_col = b_router.reshape(N_EXPERTS, 1)
    wg, wu, wd = _bf(w_exp_gate), _bf(w_exp_up), _bf(w_exp_down)

    sh1, sc1, g1, sh2, sc2, g2 = [mods(ada[0], i) for i in range(6)]
    b_g = row(b_gate_a[0])
    qp, kp, vp, rp, gp = _proj_a(xp, sh1[0], sc1[0], tpb, TM, w_main, w_gd, w_gu, b_g)
    qs, ks, vs, rs, gs = _proj_a(xs, sh1[1], sc1[1], 1, n_smp, w_main, w_gd, w_gu, b_g)
    gn = row(gn_a[0])
    ap, s_prompt = _gla_chunk(qp, kp, vp, rp, gp, gn, n_seq, t)
    as_, s_sample = _gla_step(qs, ks, vs, rs, gs, gn, state_gla[0])

    w_oa = _bf(w_o_a[0])
    lng, lnb = row(ln_mix_g[0]), row(ln_mix_b[0])
    x1p, rows_p, wts_p, meta_p, cnt_p = _post(ap, w_oa, xp, g1[0], lng, lnb, sh2[0], sc2[0], tpb, TM,
                                              wr_split, br_col)
    x1s, rows_s, wts_s, meta_s, cnt_s = _post(as_, w_oa, xs, g1[1], lng, lnb, sh2[1], sc2[1], 1, n_smp,
                                              wr_split, br_col)
    ff_sorted, pos_p, pos_s = _moe(rows_p, meta_p, cnt_p, rows_s, meta_s, cnt_s, wg, wu, wd, 0)

    shkv, sckv = mods(ada_kv, 0), mods(ada_kv, 1)
    sh1, sc1, g1b, sh2b, sc2b, g2b = [mods(ada[1], i) for i in range(6)]
    lng, lnb = row(ln_ffn_g[0]), row(ln_ffn_b[0])
    w_kvb, w_qb = _bf(w_kv), _bf(w_q_b[0])
    x2p, k_p, v_p, kb_p, vb_p, q_p = _mid(pos_p, x1p, ff_sorted, wts_p, g2[0], lng, lnb, shkv[0], sckv[0],
                                          sh1[0], sc1[0], tpb, TM, w_kvb, w_qb)
    x2s, k_s, v_s, _, _, q_s = _mid(pos_s, x1s, ff_sorted, wts_s, g2[1], lng, lnb, shkv[1], sckv[1],
                                    sh1[1], sc1[1], 1, n_smp, w_kvb, w_qb)

    lam_init = 0.8 - 0.6 * math.exp(-0.3 * 1)
    slopes = jnp.asarray([2.0 ** (-8.0 * (i + 1) / H_B) for i in range(H_B)], f32)
    nslope_rows = -jnp.repeat(slopes, 2).reshape(2 * H_B, 1)
    lam_args = (row(lam_q1[0]), row(lam_k1[0]), row(lam_q2[0]), row(lam_k2[0]), row(subln_b[0]))
    bp = _dattn(slopes, q_p, kb_p, vb_p, *lam_args, n_seq, t, lam_init)
    bs = _dattn_step(page_table, q_s, k_s, v_s, cache_k, cache_v, nslope_rows, *lam_args, past_len, lam_init)

    w_ob = _bf(w_o_b[0])
    lng, lnb = row(ln_mix_g[1]), row(ln_mix_b[1])
    x3p, rows_p, wts_p, meta_p, cnt_p = _post(bp, w_ob, x2p, g1b[0], lng, lnb, sh2b[0], sc2b[0], tpb, TM,
                                              wr_split, br_col)
    x3s, rows_s, wts_s, meta_s, cnt_s = _post(bs, w_ob, x2s, g1b[1], lng, lnb, sh2b[1], sc2b[1], 1, n_smp,
                                              wr_split, br_col)
    ff_sorted, pos_p, pos_s = _moe(rows_p, meta_p, cnt_p, rows_s, meta_s, cnt_s, wg, wu, wd, 1)
    lng, lnb = row(ln_ffn_g[1]), row(ln_ffn_b[1])
    y_p = _final(pos_p, x3p, ff_sorted, wts_p, g2b[0], lng, lnb, tpb, TM)
    y_s = _final(pos_s, x3s, ff_sorted, wts_s, g2b[1], lng, lnb, 1, n_smp)

    n_pg_p = t // PAGE
    return (y_p.reshape(n_seq, t, D), y_s.reshape(n_smp, 1, D),
            s_prompt[None], s_sample[None],
            k_p.reshape(n_seq, n_pg_p, PAGE, H_B, 2 * DH_B), v_p.reshape(n_seq, n_pg_p, PAGE, H_B, DV_B),
            k_s.reshape(n_smp, 1, H_B, 2 * DH_B), v_s.reshape(n_smp, 1, H_B, DV_B))
```

```python
import functools
import math

import jax
import jax.numpy as jnp
from jax import lax
from jax.experimental import pallas as pl
from jax.experimental.pallas import tpu as pltpu

f32 = jnp.float32
bf16 = jnp.bfloat16

D = 1024
H_A, DK_A, DV_A = 4, 128, 256
QK_A, VW_A = H_A * DK_A, H_A * DV_A
GATE_RANK = 16
GATE_TAU = 16.0
H_B, DH_B, DV_B = 8, 64, 128
N_EXPERTS, N_GROUPS, GROUP_SIZE = 16, 4, 4
D_FF = 512
N_CLASSES = N_GROUPS * 6
DEPTH = 2
ALPHA = (2.0 * DEPTH) ** 0.25
LN_EPS = 1e-5
RMS_EPS = 1e-6
NEG_BIG = -1e30
PAGE = 128

LANES = 128
VMEM_LIMIT = 48 * 1024 * 1024
TM = 512
TM_E = 256
TM_PERM = 1024
GLA_C = 64
GLA_SEQS = 4
TQ = 512
TH = TQ // 2
PAGES_PER_STEP = 8
SUB = 8
assert D == SUB * LANES


def _load_tokens(ref, n, rows_per_token):
    return jnp.concatenate([ref[pl.ds(h, n, stride=rows_per_token), :] for h in range(SUB)], axis=1)


def _store_tokens(ref, val, offset, rows_per_token):
    n = val.shape[0]
    for h in range(SUB):
        ref[pl.ds(offset + h, n, stride=rows_per_token), :] = val[:, h * LANES:(h + 1) * LANES]


def _bf(a):
    return a.astype(bf16)


def _mm(a, b):
    return jnp.dot(a, b, preferred_element_type=f32)


def _mm_nt(a, b):
    return lax.dot_general(a, b, (((1,), (1,)), ((), ())), preferred_element_type=f32)


def _mm_tn(a, b):
    return lax.dot_general(a, b, (((0,), (0,)), ((), ())), preferred_element_type=f32)


def _params(*sem):
    return pltpu.CompilerParams(dimension_semantics=sem, vmem_limit_bytes=VMEM_LIMIT)


def _layer_norm(h, g, b):
    mu = jnp.mean(h, axis=-1, keepdims=True)
    hc = h - mu
    var = jnp.mean(hc * hc, axis=-1, keepdims=True)
    return hc * lax.rsqrt(var + LN_EPS) * g + b


def _rms_norm(o, g):
    ms = jnp.mean(o * o, axis=-1, keepdims=True)
    return o * lax.rsqrt(ms + RMS_EPS) * g


def _silu(x):
    return x / (1.0 + jnp.exp(-x))


def _row_to_col(row):
    n = row.shape[1]
    eye = lax.broadcasted_iota(jnp.int32, (n, n), 0) == lax.broadcasted_iota(jnp.int32, (n, n), 1)
    return jnp.sum(jnp.where(eye, row, 0.0), axis=1, keepdims=True)


def _ada_kernel(c_ref, w_ref, b_ref, o_ref):
    o_ref[...] = _mm(_bf(c_ref[...]), _bf(w_ref[...])) + b_ref[...]


def _ada(c_all, w3, b3, tn=1024):
    n_l, _, n = w3.shape
    m = c_all.shape[0]
    return pl.pallas_call(
        _ada_kernel,
        grid=(n_l, n // tn),
        in_specs=[pl.BlockSpec((m, D), lambda l, j: (0, 0)),
                  pl.BlockSpec((None, D, tn), lambda l, j: (l, 0, j)),
                  pl.BlockSpec((None, 1, tn), lambda l, j: (l, 0, j))],
        out_specs=pl.BlockSpec((None, m, tn), lambda l, j: (l, 0, j)),
        out_shape=jax.ShapeDtypeStruct((n_l, m, n), f32),
        compiler_params=_params("parallel", "parallel"),
        name="ada",
    )(c_all, w3, b3)


def _mod_spec(mod, tiles_per_block):
    return pl.BlockSpec((None, mod.shape[1], D), lambda i, *_: (i // tiles_per_block, 0, 0))


def _row_spec(tm, width):
    return pl.BlockSpec((tm, width), lambda i, *_: (i, 0))


def _const_spec(shape):
    nd = len(shape)
    return pl.BlockSpec(shape, lambda i, *_: (0,) * nd)


def _proj_a_kernel(x_ref, sh_ref, sc_ref, w_ref, wgd_ref, wgu_ref, bg_ref,
                   q_ref, k_ref, v_ref, r_ref, g_ref):
    xm = _bf(x_ref[...] * (1.0 + sc_ref[...]) + sh_ref[...])
    q_ref[...] = _mm(xm, w_ref[:, 0:QK_A]) * (DK_A ** -0.5)
    k_ref[...] = _mm(xm, w_ref[:, QK_A:2 * QK_A])
    v_ref[...] = _mm(xm, w_ref[:, 2 * QK_A:2 * QK_A + VW_A])
    r_ref[...] = _mm(xm, w_ref[:, 2 * QK_A + VW_A:2 * QK_A + 2 * VW_A])
    gd = _mm(xm, wgd_ref[...])
    z = _mm(_bf(gd), wgu_ref[...]) + bg_ref[...]
    g_ref[...] = (jnp.minimum(z, 0.0) - jnp.log1p(jnp.exp(-jnp.abs(z)))) * (1.0 / GATE_TAU)


def _proj_a(x, sh, sc, tiles_per_block, tm, w_main, w_gd, w_gu, b_g):
    n = x.shape[0]
    outs = [jax.ShapeDtypeStruct((n, QK_A), f32), jax.ShapeDtypeStruct((n, QK_A), f32),
            jax.ShapeDtypeStruct((n, VW_A), f32), jax.ShapeDtypeStruct((n, VW_A), f32),
            jax.ShapeDtypeStruct((n, QK_A), f32)]
    return pl.pallas_call(
        _proj_a_kernel,
        grid=(n // tm,),
        in_specs=[_row_spec(tm, D), _mod_spec(sh, tiles_per_block), _mod_spec(sc, tiles_per_block),
                  _const_spec(w_main.shape), _const_spec(w_gd.shape), _const_spec(w_gu.shape),
                  _const_spec(b_g.shape)],
        out_specs=[_row_spec(tm, QK_A), _row_spec(tm, QK_A), _row_spec(tm, VW_A),
                   _row_spec(tm, VW_A), _row_spec(tm, QK_A)],
        out_shape=outs,
        compiler_params=_params("parallel"),
        name="proj_a",
    )(x, sh, sc, w_main, w_gd, w_gu, b_g)


def _gla_chunk_kernel(q_ref, k_ref, v_ref, r_ref, g_ref, gn_ref, o_ref, s_ref, st_scr, *, n_chunks):
    c = pl.program_id(1)
    C = q_ref.shape[1]

    @pl.when(c == 0)
    def _():
        st_scr[...] = jnp.zeros_like(st_scr)

    row = lax.broadcasted_iota(jnp.int32, (C, C), 0)
    col = lax.broadcasted_iota(jnp.int32, (C, C), 1)
    tril = col <= row
    tril_b = jnp.where(tril, 1.0, 0.0).astype(bf16)
    mid = C // 2 - 1
    for b in range(q_ref.shape[0]):
        g = g_ref[b]
        g1 = _bf(g)
        rem = g - g1.astype(f32)
        g2 = _bf(rem)
        g3 = _bf(rem - g2.astype(f32))
        cum = _mm(tril_b, g1) + _mm(tril_b, g2) + _mm(tril_b, g3)
        for h in range(H_A):
            ks = slice(h * DK_A, (h + 1) * DK_A)
            vs = slice(h * DV_A, (h + 1) * DV_A)
            cum_h = cum[:, ks]
            m = cum_h[mid:mid + 1, :]
            last = cum_h[C - 1:C, :]
            qh = q_ref[b, :, ks]
            kh = k_ref[b, :, ks]
            vb = _bf(v_ref[b, :, vs])
            q_in = _bf(qh * jnp.exp(cum_h))
            q_mid = _bf(qh * jnp.exp(cum_h - m))
            k_mid = _bf(kh * jnp.exp(m - cum_h))
            k_last = _bf(kh * jnp.exp(last - cum_h))
            st = st_scr[b, h]
            inter = _mm_nt(q_in, _bf(st))
            attn = jnp.where(tril, _mm_nt(q_mid, k_mid), 0.0)
            o = inter + _mm(_bf(attn), vb)
            st_new = st * jnp.exp(last) + _mm_tn(vb, k_last)
            st_scr[b, h] = st_new
            rh = r_ref[b, :, vs]
            o_ref[b, :, vs] = _bf(_rms_norm(o, gn_ref[...]) * _silu(rh))

    @pl.when(c == n_chunks - 1)
    def _():
        for b in range(q_ref.shape[0]):
            for h in range(H_A):
                s_ref[b, h] = st_scr[b, h].T


def _gla_chunk(q, k, v, r, g, gn, n_seq, t):
    n_chunks = t // GLA_C
    nb = GLA_SEQS
    rows = lambda w: pl.BlockSpec((nb, GLA_C, w), lambda b, c: (b, c, 0))
    seq = lambda a: a.reshape(n_seq, t, a.shape[1])
    o, s_fin = pl.pallas_call(
        functools.partial(_gla_chunk_kernel, n_chunks=n_chunks),
        grid=(n_seq // nb, n_chunks),
        in_specs=[rows(QK_A), rows(QK_A), rows(VW_A), rows(VW_A), rows(QK_A),
                  pl.BlockSpec((1, DV_A), lambda b, c: (0, 0))],
        out_specs=[rows(VW_A), pl.BlockSpec((nb, H_A, DK_A, DV_A), lambda b, c: (b, 0, 0, 0))],
        out_shape=[jax.ShapeDtypeStruct((n_seq, t, VW_A), bf16),
                   jax.ShapeDtypeStruct((n_seq, H_A, DK_A, DV_A), f32)],
        scratch_shapes=[pltpu.VMEM((nb, H_A, DV_A, DK_A), f32)],
        compiler_params=_params("parallel", "arbitrary"),
        name="gla_chunk",
    )(seq(q), seq(k), seq(v), seq(r), seq(g), gn)
    return o.reshape(n_seq * t, VW_A), s_fin


def _gla_step_kernel(q_ref, k_ref, v_ref, r_ref, g_ref, gn_ref, s_ref, o_ref, sn_ref):
    for h in range(H_A):
        ks = slice(h * DK_A, (h + 1) * DK_A)
        vs = slice(h * DV_A, (h + 1) * DV_A)
        q_col = _row_to_col(q_ref[:, ks])
        k_col = _row_to_col(k_ref[:, ks])
        eg_col = _row_to_col(jnp.exp(g_ref[:, ks]))
        s_new = eg_col * s_ref[h] + k_col * v_ref[:, vs]
        sn_ref[h] = s_new
        o = jnp.sum(q_col * s_new, axis=0, keepdims=True)
        rh = r_ref[:, vs]
        o_ref[:, vs] = _rms_norm(o, gn_ref[...]) * _silu(rh)


def _gla_step(q, k, v, r, g, gn, state):
    n = q.shape[0]
    vec = lambda w: pl.BlockSpec((None, 1, w), lambda b: (b, 0, 0))
    st = pl.BlockSpec((None, H_A, DK_A, DV_A), lambda b: (b, 0, 0, 0))
    o, s_new = pl.pallas_call(
        _gla_step_kernel,
        grid=(n,),
        in_specs=[vec(QK_A), vec(QK_A), vec(VW_A), vec(VW_A), vec(QK_A),
                  pl.BlockSpec((1, DV_A), lambda b: (0, 0)), st],
        out_specs=[vec(VW_A), st],
        out_shape=[jax.ShapeDtypeStruct((n, 1, VW_A), f32),
                   jax.ShapeDtypeStruct((n, H_A, DK_A, DV_A), f32)],
        compiler_params=_params("parallel"),
        name="gla_step",
    )(q.reshape(n, 1, QK_A), k.reshape(n, 1, QK_A), v.reshape(n, 1, VW_A), r.reshape(n, 1, VW_A),
      g.reshape(n, 1, QK_A), gn, state)
    return o.reshape(n, VW_A), s_new


def _route(probs):
    p = [probs[e:e + 1, :] for e in range(N_EXPERTS)]
    scores = []
    for gi in range(N_GROUPS):
        a0, a1, a2, a3 = p[4 * gi:4 * gi + 4]
        hi1, lo1 = jnp.maximum(a0, a1), jnp.minimum(a0, a1)
        hi2, lo2 = jnp.maximum(a2, a3), jnp.minimum(a2, a3)
        scores.append(jnp.maximum(hi1, hi2) + jnp.maximum(jnp.minimum(hi1, hi2), jnp.maximum(lo1, lo2)))
    best = scores[0]
    gidx = jnp.zeros(best.shape, jnp.int32)
    for gi in range(1, N_GROUPS):
        upd = scores[gi] > best
        best = jnp.where(upd, scores[gi], best)
        gidx = jnp.where(upd, gi, gidx)
    a = [jnp.where(gidx == 0, p[j], jnp.where(gidx == 1, p[4 + j], jnp.where(gidx == 2, p[8 + j], p[12 + j])))
         for j in range(GROUP_SIZE)]
    v1 = a[0]
    i1 = jnp.zeros(best.shape, jnp.int32)
    for j in range(1, GROUP_SIZE):
        upd = a[j] > v1
        v1 = jnp.where(upd, a[j], v1)
        i1 = jnp.where(upd, j, i1)
    v2 = jnp.full(best.shape, -1.0, f32)
    i2 = jnp.zeros(best.shape, jnp.int32)
    for j in range(GROUP_SIZE):
        upd = jnp.logical_and(i1 != j, a[j] > v2)
        v2 = jnp.where(upd, a[j], v2)
        i2 = jnp.where(upd, j, i2)
    tot = v1 + v2
    w1 = v1 / tot
    w2 = v2 / tot
    first_low = i1 < i2
    lo = jnp.minimum(i1, i2)
    hi = jnp.maximum(i1, i2)
    pair = lo * 3 - ((lo * (lo - 1)) >> 1) + hi - lo - 1
    cls = gidx * 6 + pair
    return cls, jnp.where(first_low, w1, w2), jnp.where(first_low, w2, w1)


def _post_kernel(a_ref, w_ref, x_ref, g1_ref, lng_ref, lnb_ref, sh2_ref, sc2_ref, wr_ref, br_ref,
                 x1_ref, rows_ref, wts_ref, meta_ref, cnt_ref, cnt_scr):
    i = pl.program_id(0)
    tm = x_ref.shape[0]

    @pl.when(i == 0)
    def _():
        cnt_scr[...] = jnp.zeros_like(cnt_scr)

    mix = _mm(_bf(a_ref[...]), w_ref[...])
    x1 = _layer_norm(ALPHA * x_ref[...] + g1_ref[...] * mix, lng_ref[...], lnb_ref[...])
    x1_ref[...] = x1
    xm = x1 * (1.0 + sc2_ref[...]) + sh2_ref[...]
    _store_tokens(rows_ref, xm, 0, SUB)

    x_hi = _bf(xm)
    x_lo = _bf(xm - x_hi.astype(f32))
    wr = wr_ref[...]
    l_hi = _mm_nt(wr, x_hi)
    l_lo = _mm_nt(wr[0:N_EXPERTS], x_lo)
    logits = l_hi[0:N_EXPERTS] + l_hi[N_EXPERTS:2 * N_EXPERTS] + l_lo + br_ref[...]
    mx = jnp.max(logits, axis=0, keepdims=True)
    ex = jnp.exp(logits - mx)
    probs = ex / jnp.sum(ex, axis=0, keepdims=True)
    cls, w_lo, w_hi = _route(probs)

    crow = lax.broadcasted_iota(jnp.int32, (32, tm), 0)
    onehot = crow == cls
    oh = jnp.where(onehot, 1.0, 0.0)
    before = lax.broadcasted_iota(jnp.int32, (tm, tm), 0) < lax.broadcasted_iota(jnp.int32, (tm, tm), 1)
    prefix = _mm(_bf(oh), jnp.where(before, 1.0, 0.0).astype(bf16))
    carry = cnt_scr[...]
    rank = jnp.sum(jnp.where(onehot, prefix + carry, 0.0), axis=0, keepdims=True)
    cnt_new = carry + jnp.sum(oh, axis=1, keepdims=True)
    cnt_scr[...] = cnt_new
    cnt_ref[...] = jnp.broadcast_to(cnt_new, cnt_ref.shape)

    mrow = lax.broadcasted_iota(jnp.int32, (8, tm), 0)
    meta_ref[...] = jnp.where(mrow == 0, cls, jnp.where(mrow == 1, rank.astype(jnp.int32), 0))

    lane = lax.broadcasted_iota(jnp.int32, (tm, LANES), 1)
    wts_ref[...] = jnp.where(lane == 0, _row_to_col(w_lo), jnp.where(lane == 1, _row_to_col(w_hi), 0.0))


def _post(a, w_o, x, g1, lng, lnb, sh2, sc2, tiles_per_block, tm, wr_t, br_col):
    n = x.shape[0]
    mod = lambda arr: _mod_spec(arr, tiles_per_block)
    return pl.pallas_call(
        _post_kernel,
        grid=(n // tm,),
        in_specs=[_row_spec(tm, D), _const_spec(w_o.shape), _row_spec(tm, D), mod(g1),
                  _const_spec(lng.shape), _const_spec(lnb.shape), mod(sh2), mod(sc2),
                  _const_spec(wr_t.shape), _const_spec(br_col.shape)],
        out_specs=[_row_spec(tm, D), _row_spec(tm * SUB, LANES), _row_spec(tm, LANES),
                   pl.BlockSpec((8, tm), lambda i: (0, i)),
                   pl.BlockSpec((32, LANES), lambda i: (0, 0))],
        out_shape=[jax.ShapeDtypeStruct((n, D), f32), jax.ShapeDtypeStruct((n * SUB, LANES), f32),
                   jax.ShapeDtypeStruct((n, LANES), f32), jax.ShapeDtypeStruct((8, n), jnp.int32),
                   jax.ShapeDtypeStruct((32, LANES), f32)],
        scratch_shapes=[pltpu.VMEM((32, 1), f32)],
        compiler_params=_params("arbitrary"),
        name="post",
    )(a, w_o, x, g1, lng, lnb, sh2, sc2, wr_t, br_col)


def _scatter_kernel(pos_ref, src_ref, dst_in_ref, dst_ref, sem, *, tm, rows):
    del dst_in_ref
    base = pl.program_id(0) * tm

    def token_copy(r):
        slot = pl.multiple_of(pos_ref[base + r] * rows, rows)
        return pltpu.make_async_copy(src_ref.at[pl.ds(pl.multiple_of(r * rows, rows), rows)],
                                     dst_ref.at[pl.ds(slot, rows)], sem)

    def start(r, carry):
        token_copy(r).start()
        return carry

    def wait(r, carry):
        token_copy(r).wait()
        return carry

    lax.fori_loop(0, tm, start, 0, unroll=8)
    lax.fori_loop(0, tm, wait, 0, unroll=8)


def _scatter_rows(pos, src, dst, rows):
    n = pos.shape[0]
    tm = min(TM_PERM, n)
    return pl.pallas_call(
        functools.partial(_scatter_kernel, tm=tm, rows=rows),
        grid_spec=pltpu.PrefetchScalarGridSpec(
            num_scalar_prefetch=1, grid=(n // tm,),
            in_specs=[pl.BlockSpec((tm * rows, LANES), lambda t, pos: (t, 0)),
                      pl.BlockSpec(memory_space=pl.ANY)],
            out_specs=pl.BlockSpec(memory_space=pl.ANY),
            scratch_shapes=[pltpu.SemaphoreType.DMA(())]),
        out_shape=jax.ShapeDtypeStruct(dst.shape, dst.dtype),
        input_output_aliases={2: 0},
        compiler_params=_params("arbitrary"),
        name="scatter_rows",
    )(pos, src, dst)


def _gather_tokens(pos_ref, src_ref, buf_ref, sem, tile, tm, rows, wait):
    base = tile * tm

    def token_copy(r):
        slot = pl.multiple_of(pos_ref[base + r] * rows, rows)
        return pltpu.make_async_copy(src_ref.at[pl.ds(slot, rows)],
                                     buf_ref.at[pl.ds(pl.multiple_of(r * rows, rows), rows)], sem)

    def step(r, carry):
        if wait:
            token_copy(r).wait()
        else:
            token_copy(r).start()
        return carry

    lax.fori_loop(0, tm, step, 0, unroll=8)


def _prefetched_tokens(pos_ref, src_ref, buf_ref, sem_ref, tm, rows):
    i = pl.program_id(0)
    slot = i % 2

    @pl.when(i == 0)
    def _():
        _gather_tokens(pos_ref, src_ref, buf_ref.at[0], sem_ref.at[0], 0, tm, rows, wait=False)

    @pl.when(i + 1 < pl.num_programs(0))
    def _():
        _gather_tokens(pos_ref, src_ref, buf_ref.at[1 - slot], sem_ref.at[1 - slot], i + 1, tm, rows, wait=False)

    _gather_tokens(pos_ref, src_ref, buf_ref.at[slot], sem_ref.at[slot], i, tm, rows, wait=True)
    return buf_ref.at[slot]


def _experts_kernel(elo_ref, ehi_ref, valid_ref, rows_ref,
                    wg_lo, wu_lo, wd_lo, wg_hi, wu_hi, wd_hi, o_ref):
    t = pl.program_id(0)

    @pl.when(valid_ref[t] == 1)
    def _():
        xb = _bf(_load_tokens(rows_ref, TM_E, SUB))

        def ffn(wg, wu, wd):
            hid = _silu(_mm(xb, wg[...])) * _mm(xb, wu[...])
            return _mm(_bf(hid), wd[...])

        _store_tokens(o_ref, ffn(wg_lo, wu_lo, wd_lo), 0, 2 * SUB)
        _store_tokens(o_ref, ffn(wg_hi, wu_hi, wd_hi), SUB, 2 * SUB)

    @pl.when(valid_ref[t] == 0)
    def _():
        o_ref[...] = jnp.zeros_like(o_ref)


def _experts(e_lo, e_hi, valid, rows, wg, wu, wd, layer):
    n_tiles = rows.shape[0] // (TM_E * SUB)
    up = lambda sel: pl.BlockSpec((None, None, D, D_FF), lambda t, lo, hi, v: (layer, sel(lo, hi)[t], 0, 0))
    down = lambda sel: pl.BlockSpec((None, None, D_FF, D), lambda t, lo, hi, v: (layer, sel(lo, hi)[t], 0, 0))
    first = lambda lo, hi: lo
    second = lambda lo, hi: hi
    return pl.pallas_call(
        _experts_kernel,
        grid_spec=pltpu.PrefetchScalarGridSpec(
            num_scalar_prefetch=3, grid=(n_tiles,),
            in_specs=[pl.BlockSpec((TM_E * SUB, LANES), lambda t, lo, hi, v: (t, 0)),
                      up(first), up(first), down(first), up(second), up(second), down(second)],
            out_specs=pl.BlockSpec((TM_E * 2 * SUB, LANES), lambda t, lo, hi, v: (t, 0))),
        out_shape=jax.ShapeDtypeStruct((2 * rows.shape[0], LANES), f32),
        compiler_params=_params("parallel"),
        name="experts",
    )(e_lo, e_hi, valid, rows, wg, wu, wd, wg, wu, wd)


def _moe(rows_p, meta_p, cnt_p, rows_s, meta_s, cnt_s, wg, wu, wd, layer):
    n_p, n_s = meta_p.shape[1], meta_s.shape[1]
    n_tiles = -(-(n_p + n_s) // TM_E) + N_CLASSES
    cp = cnt_p[:N_CLASSES, 0].astype(jnp.int32)
    cs = cnt_s[:N_CLASSES, 0].astype(jnp.int32)
    tiles_c = (cp + cs + TM_E - 1) // TM_E
    tile_end = jnp.cumsum(tiles_c)
    row_off = (tile_end - tiles_c) * TM_E
    pos_p = row_off[meta_p[0]] + meta_p[1]
    pos_s = row_off[meta_s[0]] + cp[meta_s[0]] + meta_s[1]
    t_idx = jnp.arange(n_tiles, dtype=jnp.int32)
    valid = (t_idx < tile_end[-1]).astype(jnp.int32)
    tile_cls = jnp.minimum(jnp.sum((tile_end[None, :] <= t_idx[:, None]).astype(jnp.int32), axis=1), N_CLASSES - 1)
    pairs = [(a, b) for a in range(GROUP_SIZE) for b in range(a + 1, GROUP_SIZE)]
    lo_tab = jnp.asarray([gi * GROUP_SIZE + a for gi in range(N_GROUPS) for a, _ in pairs], jnp.int32)
    hi_tab = jnp.asarray([gi * GROUP_SIZE + b for gi in range(N_GROUPS) for _, b in pairs], jnp.int32)
    e_lo, e_hi = lo_tab[tile_cls], hi_tab[tile_cls]

    buf = jnp.zeros((n_tiles * TM_E * SUB, LANES), f32)
    buf = _scatter_rows(pos_p, rows_p, buf, SUB)
    buf = _scatter_rows(pos_s, rows_s, buf, SUB)
    ff_sorted = _experts(e_lo, e_hi, valid, buf, wg, wu, wd, layer)
    return ff_sorted, pos_p, pos_s


def _ffn_out(pos_ref, sorted_ref, wts_ref, buf_ref, sem_ref, n):
    pair_ref = _prefetched_tokens(pos_ref, sorted_ref, buf_ref, sem_ref, n, 2 * SUB)
    wts = wts_ref[...]
    lo = jnp.concatenate([pair_ref[pl.ds(h, n, stride=2 * SUB), :] for h in range(SUB)], axis=1)
    hi = jnp.concatenate([pair_ref[pl.ds(SUB + h, n, stride=2 * SUB), :] for h in range(SUB)], axis=1)
    return wts[:, 0:1] * lo + wts[:, 1:2] * hi


def _mid_kernel(pos_ref, x1_ref, sorted_ref, wts_ref, g2_ref, lng_ref, lnb_ref, shkv_ref, sckv_ref, shq_ref,
                scq_ref, wkv_ref, wq_ref, x2_ref, k_ref, v_ref, kb_ref, vb_ref, q_ref, buf_ref, sem_ref):
    ff = _ffn_out(pos_ref, sorted_ref, wts_ref, buf_ref, sem_ref, x1_ref.shape[0])
    x2 = _layer_norm(ALPHA * x1_ref[...] + g2_ref[...] * ff, lng_ref[...], lnb_ref[...])
    x2_ref[...] = x2
    xkv = _bf(x2 * (1.0 + sckv_ref[...]) + shkv_ref[...])
    kk = _mm(xkv, wkv_ref[:, 0:D])
    vv = _mm(xkv, wkv_ref[:, D:2 * D])
    _store_tokens(k_ref, kk, 0, SUB)
    _store_tokens(v_ref, vv, 0, SUB)
    kb_ref[...] = _bf(kk)
    vb_ref[...] = _bf(vv)
    xq = _bf(x2 * (1.0 + scq_ref[...]) + shq_ref[...])
    q_ref[...] = _bf(_mm(xq, wq_ref[...]) * (DH_B ** -0.5))


def _pair_scratch(tm):
    return [pltpu.VMEM((2, tm * 2 * SUB, LANES), f32), pltpu.SemaphoreType.DMA((2,))]


def _mid(pos, x1, ff_sorted, wts, g2, lng, lnb, shkv, sckv, shq, scq, tiles_per_block, tm, w_kv, w_q):
    n = x1.shape[0]
    mod = lambda arr: _mod_spec(arr, tiles_per_block)
    sds = lambda dt: jax.ShapeDtypeStruct((n, D), dt)
    kv_sds = jax.ShapeDtypeStruct((n * SUB, LANES), f32)
    return pl.pallas_call(
        _mid_kernel,
        grid_spec=pltpu.PrefetchScalarGridSpec(
            num_scalar_prefetch=1, grid=(n // tm,),
            in_specs=[_row_spec(tm, D), pl.BlockSpec(memory_space=pl.ANY), _row_spec(tm, LANES), mod(g2),
                      _const_spec(lng.shape), _const_spec(lnb.shape),
                      mod(shkv), mod(sckv), mod(shq), mod(scq), _const_spec(w_kv.shape), _const_spec(w_q.shape)],
            out_specs=[_row_spec(tm, D), _row_spec(tm * SUB, LANES), _row_spec(tm * SUB, LANES),
                       _row_spec(tm, D), _row_spec(tm, D), _row_spec(tm, D)],
            scratch_shapes=_pair_scratch(tm)),
        out_shape=[sds(f32), kv_sds, kv_sds, sds(bf16), sds(bf16), sds(bf16)],
        compiler_params=_params("arbitrary"),
        name="mid",
    )(pos, x1, ff_sorted, wts, g2, lng, lnb, shkv, sckv, shq, scq, w_kv, w_q)


def _lambda(lq1_ref, lk1_ref, lq2_ref, lk2_ref, lam_init):
    s1 = jnp.sum(lq1_ref[...] * lk1_ref[...], axis=1, keepdims=True)
    s2 = jnp.sum(lq2_ref[...] * lk2_ref[...], axis=1, keepdims=True)
    return jnp.exp(s1) - jnp.exp(s2) + lam_init


def _online_update(s, v, m_ref, l_ref, acc_ref):
    m_old = m_ref[...]
    m_new = jnp.maximum(m_old, jnp.max(s, axis=1, keepdims=True))
    p = jnp.exp(s - jnp.tile(m_new, (1, s.shape[1] // LANES)))
    corr = jnp.exp(m_old - m_new)
    l_ref[...] = l_ref[...] * corr + jnp.sum(p, axis=1, keepdims=True)
    acc_ref[...] = acc_ref[...] * jnp.tile(corr, (1, acc_ref.shape[1] // LANES)) + _mm(p.astype(v.dtype), v)
    m_ref[...] = m_new


def _dattn_kernel(slope_ref, q_ref, k_ref, v_ref, lq1_ref, lk1_ref, lq2_ref, lk2_ref, subln_ref,
                  o_ref, m_scr, l_scr, acc_scr, *, lam_init):
    t = q_ref.shape[0]
    slope = slope_ref[pl.program_id(1)]
    lam = _lambda(lq1_ref, lk1_ref, lq2_ref, lk2_ref, lam_init)
    first_map = lax.broadcasted_iota(jnp.int32, (1, 2 * DH_B), 1) < DH_B
    row = lax.broadcasted_iota(jnp.int32, (2 * TQ, 1), 0)
    rel_q = (row & (TH - 1)) + ((row >> 9) << 8)
    assert TH == 256 and TQ == 512

    def attend(qs, rel_rows, rows, q0, k0, nk, masked):
        s = _mm_nt(qs, k_ref[pl.ds(k0, nk), :])
        rel_k = lax.broadcasted_iota(jnp.int32, (1, nk), 1) + (k0 - q0)
        s = s + slope * rel_k.astype(f32)
        if masked:
            s = jnp.where(rel_k <= rel_rows, s, NEG_BIG)
        _online_update(s, v_ref[pl.ds(k0, nk), :], m_scr.at[rows], l_scr.at[rows], acc_scr.at[rows])

    for i in range(t // TQ):
        q0 = i * TQ
        m_scr[...] = jnp.full(m_scr.shape, NEG_BIG, f32)
        l_scr[...] = jnp.zeros_like(l_scr)
        acc_scr[...] = jnp.zeros_like(acc_scr)
        blocks = []
        for half in range(2):
            q = q_ref[q0 + half * TH:q0 + (half + 1) * TH, :]
            zero = jnp.zeros_like(q)
            blocks += [jnp.where(first_map, q, zero), jnp.where(first_map, zero, q)]
        qs = jnp.concatenate(blocks, axis=0)
        every = pl.ds(0, 2 * TQ)
        second_half = pl.ds(TQ, TQ)

        def below_diagonal(j, carry):
            attend(qs, rel_q, every, q0, pl.multiple_of(j * TQ, TQ), TQ, False)
            return carry

        lax.fori_loop(0, i, below_diagonal, 0)
        attend(qs, rel_q, every, q0, q0, TH, True)
        attend(qs[TQ:2 * TQ], rel_q[TQ:2 * TQ], second_half, q0, q0 + TH, TH, True)

        for half in range(2):
            r0, r1 = 2 * half * TH, (2 * half + 1) * TH
            o1 = acc_scr[r0:r0 + TH] / l_scr[r0:r0 + TH]
            o2 = acc_scr[r1:r1 + TH] / l_scr[r1:r1 + TH]
            o = o1 - lam * o2
            o_ref[q0 + half * TH:q0 + (half + 1) * TH, :] = _bf(_rms_norm(o, subln_ref[...]) * (1.0 - lam_init))


def _dattn(slopes, q, k, v, lq1, lk1, lq2, lk2, subln, n_seq, t, lam_init):
    head = pl.BlockSpec((t, 2 * DH_B), lambda b, h: (b, h))
    small = lambda w: pl.BlockSpec((1, w), lambda b, h: (0, 0))
    return pl.pallas_call(
        functools.partial(_dattn_kernel, lam_init=lam_init),
        grid=(n_seq, H_B),
        in_specs=[pl.BlockSpec(memory_space=pltpu.SMEM), head, head, head,
                  small(DH_B), small(DH_B), small(DH_B), small(DH_B), small(DV_B)],
        out_specs=head,
        out_shape=jax.ShapeDtypeStruct((n_seq * t, H_B * DV_B), bf16),
        scratch_shapes=[pltpu.VMEM((2 * TQ, LANES), f32), pltpu.VMEM((2 * TQ, LANES), f32),
                        pltpu.VMEM((2 * TQ, DV_B), f32)],
        compiler_params=_params("parallel", "parallel"),
        name="dattn_prompt",
    )(slopes, q, k, v, lq1, lk1, lq2, lk2, subln)


def _dattn_step_kernel(pt_ref, q_ref, kn_ref, vn_ref, nslope_ref, lq1_ref, lk1_ref, lq2_ref, lk2_ref,
                       subln_ref, *rest, past_len, lam_init):
    n_pg = PAGES_PER_STEP
    k_refs, v_refs = rest[0:n_pg], rest[n_pg:2 * n_pg]
    o_ref, qbd_scr, m_scr, l_scr, acc_scr = rest[2 * n_pg:]
    j = pl.program_id(1)
    rows = 2 * H_B

    @pl.when(j == 0)
    def _():
        lane = lax.broadcasted_iota(jnp.int32, (rows, D), 1)
        rowi = lax.broadcasted_iota(jnp.int32, (rows, D), 0)
        q = jnp.broadcast_to(q_ref[...].astype(f32), (rows, D))
        qbd_scr[...] = jnp.where((lane >> 6) == rowi, q, 0.0)
        m_scr[...] = jnp.full(m_scr.shape, NEG_BIG, f32)
        l_scr[...] = jnp.zeros_like(l_scr)
        acc_scr[...] = jnp.zeros_like(acc_scr)

    qbd = qbd_scr[...]
    s = jnp.concatenate([_mm_nt(qbd, _load_tokens(k_refs[u], PAGE, SUB)) for u in range(n_pg)], axis=1)
    kpos = j * (n_pg * PAGE) + lax.broadcasted_iota(jnp.int32, (1, n_pg * PAGE), 1)
    s = s + nslope_ref[...] * (past_len - kpos).astype(f32)
    m_old = m_scr[...]
    m_new = jnp.maximum(m_old, jnp.max(s, axis=1, keepdims=True))
    p = jnp.exp(s - jnp.tile(m_new, (1, n_pg)))
    corr = jnp.exp(m_old - m_new)
    l_scr[...] = l_scr[...] * corr + jnp.sum(p, axis=1, keepdims=True)
    pv = _mm(p[:, 0:PAGE], _load_tokens(v_refs[0], PAGE, SUB))
    for u in range(1, n_pg):
        pv = pv + _mm(p[:, u * PAGE:(u + 1) * PAGE], _load_tokens(v_refs[u], PAGE, SUB))
    acc_scr[...] = acc_scr[...] * jnp.tile(corr, (1, SUB)) + pv
    m_scr[...] = m_new

    @pl.when(j == pl.num_programs(1) - 1)
    def _():
        kn = jnp.concatenate([kn_ref[h:h + 1, :] for h in range(H_B)], axis=1)
        vn = jnp.concatenate([vn_ref[h:h + 1, :] for h in range(H_B)], axis=1)
        kn_b = _bf(kn).astype(f32)
        vn_b = _bf(vn).astype(f32)
        s = jnp.sum(qbd * kn_b, axis=1, keepdims=True)
        m_old = m_scr[...]
        m_new = jnp.maximum(m_old, s)
        p = jnp.exp(s - m_new)
        corr = jnp.exp(m_old - m_new)
        l_fin = l_scr[...] * corr + p
        acc = acc_scr[...] * jnp.tile(corr, (1, SUB)) + _bf(jnp.tile(p, (1, SUB))).astype(f32) * vn_b
        lane = lax.broadcasted_iota(jnp.int32, (rows, D), 1)
        rowi = lax.broadcasted_iota(jnp.int32, (rows, D), 0)
        a = jnp.where((lane >> 7) == (rowi >> 1), acc / jnp.tile(l_fin, (1, SUB)), 0.0)
        a1 = jnp.sum(jnp.where((rowi & 1) == 0, a, 0.0), axis=0, keepdims=True)
        a2 = jnp.sum(jnp.where((rowi & 1) == 1, a, 0.0), axis=0, keepdims=True)
        o = a1 - _lambda(lq1_ref, lk1_ref, lq2_ref, lk2_ref, lam_init) * a2
        for h in range(H_B):
            hs = slice(h * DV_B, (h + 1) * DV_B)
            o_ref[:, hs] = _rms_norm(o[:, hs], subln_ref[...]) * (1.0 - lam_init)


def _dattn_step(page_table, q, k_new, v_new, cache_k, cache_v, nslope_rows, lq1, lk1, lq2, lk2, subln,
                past_len, lam_init):
    n, n_pages = page_table.shape
    n_pool = cache_k.shape[0]
    ck = cache_k.reshape(n_pool, PAGE * H_B, DV_B)
    cv = cache_v.reshape(n_pool, PAGE * H_B, DV_B)
    n_pg = PAGES_PER_STEP
    vec = pl.BlockSpec((None, 1, D), lambda b, j, pt: (b, 0, 0))
    new_kv = pl.BlockSpec((SUB, LANES), lambda b, j, pt: (b, 0))
    const = lambda shape: pl.BlockSpec(shape, lambda b, j, pt: (0,) * len(shape))

    def page(u):
        return pl.BlockSpec((None, PAGE * H_B, DV_B), lambda b, j, pt: (pt[b, j * n_pg + u], 0, 0))

    pages = [page(u) for u in range(n_pg)]
    out = pl.pallas_call(
        functools.partial(_dattn_step_kernel, past_len=past_len, lam_init=lam_init),
        grid_spec=pltpu.PrefetchScalarGridSpec(
            num_scalar_prefetch=1, grid=(n, n_pages // n_pg),
            in_specs=[vec, new_kv, new_kv, const((2 * H_B, 1)), const((1, DH_B)), const((1, DH_B)),
                      const((1, DH_B)), const((1, DH_B)), const((1, DV_B))] + pages + pages,
            out_specs=vec,
            scratch_shapes=[pltpu.VMEM((2 * H_B, D), f32), pltpu.VMEM((2 * H_B, LANES), f32),
                            pltpu.VMEM((2 * H_B, LANES), f32), pltpu.VMEM((2 * H_B, D), f32)]),
        out_shape=jax.ShapeDtypeStruct((n, 1, D), f32),
        compiler_params=_params("parallel", "arbitrary"),
        name="dattn_step",
    )(page_table, q.reshape(n, 1, D), k_new, v_new, nslope_rows,
      lq1, lk1, lq2, lk2, subln, *([ck] * n_pg), *([cv] * n_pg))
    return out.reshape(n, D)


def _final_kernel(pos_ref, x_ref, sorted_ref, wts_ref, g2_ref, lng_ref, lnb_ref, y_ref, buf_ref, sem_ref):
    ff = _ffn_out(pos_ref, sorted_ref, wts_ref, buf_ref, sem_ref, x_ref.shape[0])
    y_ref[...] = _layer_norm(ALPHA * x_ref[...] + g2_ref[...] * ff, lng_ref[...], lnb_ref[...])


def _final(pos, x, ff_sorted, wts, g2, lng, lnb, tiles_per_block, tm):
    n = x.shape[0]
    return pl.pallas_call(
        _final_kernel,
        grid_spec=pltpu.PrefetchScalarGridSpec(
            num_scalar_prefetch=1, grid=(n // tm,),
            in_specs=[_row_spec(tm, D), pl.BlockSpec(memory_space=pl.ANY), _row_spec(tm, LANES),
                      _mod_spec(g2, tiles_per_block), _const_spec(lng.shape), _const_spec(lnb.shape)],
            out_specs=_row_spec(tm, D),
            scratch_shapes=_pair_scratch(tm)),
        out_shape=jax.ShapeDtypeStruct((n, D), f32),
        compiler_params=_params("arbitrary"),
        name="final",
    )(pos, x, ff_sorted, wts, g2, lng, lnb)


def kernel(x_prompt, x_sample, c_prompt, c_sample, state_gla, cache_k, cache_v, page_table, w_ada, b_ada,
           ln_mix_g, ln_mix_b, ln_ffn_g, ln_ffn_b, w_in_a, w_gate_up_a, b_gate_a, gn_a, w_o_a, w_ada_kv,
           b_ada_kv, w_kv, w_q_b, lam_q1, lam_k1, lam_q2, lam_k2, subln_b, w_o_b, w_router, b_router,
           w_exp_gate, w_exp_up, w_exp_down):
    n_seq, t, _ = x_prompt.shape
    n_smp = x_sample.shape[0]
    n_p = n_seq * t
    past_len = page_table.shape[1] * PAGE
    xp = x_prompt.reshape(n_p, D)
    xs = x_sample.reshape(n_smp, D)
    tpb = t // TM

    c_all = jnp.concatenate([c_prompt, c_sample], axis=0)
    ada = _ada(c_all, w_ada, b_ada.reshape(DEPTH, 1, 6 * D))
    ada_kv = _ada(c_all, w_ada_kv.reshape(1, D, 2 * D), b_ada_kv.reshape(1, 1, 2 * D))[0]

    def mods(table, idx):
        part = table[:, idx * D:(idx + 1) * D]
        return part[:n_seq].reshape(n_seq, 1, D), part[n_seq:].reshape(1, n_smp, D)

    row = lambda a: a.reshape(1, -1)

    w_in = w_in_a[0]
    w_main = _bf(w_in[:, :2 * QK_A + 2 * VW_A])
    w_gd = _bf(jnp.pad(w_in[:, 2 * QK_A + 2 * VW_A:], ((0, 0), (0, LANES - GATE_RANK))))
    w_gu = _bf(jnp.pad(w_gate_up_a[0], ((0, LANES - GATE_RANK), (0, 0))))
    wr_t = w_router.T
    wr_hi = _bf(wr_t)
    wr_split = jnp.concatenate([wr_hi, _bf(wr_t - wr_hi.astype(f32))], axis=0)
    br_col = b_router.reshape(N_EXPERTS, 1)
    wg, wu, wd = _bf(w_exp_gate), _bf(w_exp_up), _bf(w_exp_down)

    sh1, sc1, g1, sh2, sc2, g2 = [mods(ada[0], i) for i in range(6)]
    b_g = row(b_gate_a[0])
    qp, kp, vp, rp, gp = _proj_a(xp, sh1[0], sc1[0], tpb, TM, w_main, w_gd, w_gu, b_g)
    qs, ks, vs, rs, gs = _proj_a(xs, sh1[1], sc1[1], 1, n_smp, w_main, w_gd, w_gu, b_g)
    gn = row(gn_a[0])
    ap, s_prompt = _gla_chunk(qp, kp, vp, rp, gp, gn, n_seq, t)
    as_, s_sample = _gla_step(qs, ks, vs, rs, gs, gn, state_gla[0])

    w_oa = _bf(w_o_a[0])
    lng, lnb = row(ln_mix_g[0]), row(ln_mix_b[0])
    x1p, rows_p, wts_p, meta_p, cnt_p = _post(ap, w_oa, xp, g1[0], lng, lnb, sh2[0], sc2[0], tpb, TM,
                                              wr_split, br_col)
    x1s, rows_s, wts_s, meta_s, cnt_s = _post(as_, w_oa, xs, g1[1], lng, lnb, sh2[1], sc2[1], 1, n_smp,
                                              wr_split, br_col)
    ff_sorted, pos_p, pos_s = _moe(rows_p, meta_p, cnt_p, rows_s, meta_s, cnt_s, wg, wu, wd, 0)

    shkv, sckv = mods(ada_kv, 0), mods(ada_kv, 1)
    sh1, sc1, g1b, sh2b, sc2b, g2b = [mods(ada[1], i) for i in range(6)]
    lng, lnb = row(ln_ffn_g[0]), row(ln_ffn_b[0])
    w_kvb, w_qb = _bf(w_kv), _bf(w_q_b[0])
    x2p, k_p, v_p, kb_p, vb_p, q_p = _mid(pos_p, x1p, ff_sorted, wts_p, g2[0], lng, lnb, shkv[0], sckv[0],
                                          sh1[0], sc1[0], tpb, TM, w_kvb, w_qb)
    x2s, k_s, v_s, _, _, q_s = _mid(pos_s, x1s, ff_sorted, wts_s, g2[1], lng, lnb, shkv[1], sckv[1],
                                    sh1[1], sc1[1], 1, n_smp, w_kvb, w_qb)

    lam_init = 0.8 - 0.6 * math.exp(-0.3 * 1)
    slopes = jnp.asarray([2.0 ** (-8.0 * (i + 1) / H_B) for i in range(H_B)], f32)
    nslope_rows = -jnp.repeat(slopes, 2).reshape(2 * H_B, 1)
    lam_args = (row(lam_q1[0]), row(lam_k1[0]), row(lam_q2[0]), row(lam_k2[0]), row(subln_b[0]))
    bp = _dattn(slopes, q_p, kb_p, vb_p, *lam_args, n_seq, t, lam_init)
    bs = _dattn_step(page_table, q_s, k_s, v_s, cache_k, cache_v, nslope_rows, *lam_args, past_len, lam_init)

    w_ob = _bf(w_o_b[0])
    lng, lnb = row(ln_mix_g[1]), row(ln_mix_b[1])
    x3p, rows_p, wts_p, meta_p, cnt_p = _post(bp, w_ob, x2p, g1b[0], lng, lnb, sh2b[0], sc2b[0], tpb, TM,
                                              wr_split, br_col)
    x3s, rows_s, wts_s, meta_s, cnt_s = _post(bs, w_ob, x2s, g1b[1], lng, lnb, sh2b[1], sc2b[1], 1, n_smp,
                                              wr_split, br_col)
    ff_sorted, pos_p, pos_s = _moe(rows_p, meta_p, cnt_p, rows_s, meta_s, cnt_s, wg, wu, wd, 1)
    lng, lnb = row(ln_ffn_g[1]), row(ln_ffn_b[1])
    y_p = _final(pos_p, x3p, ff_sorted, wts_p, g2b[0], lng, lnb, tpb, TM)
    y_s = _final(pos_s, x3s, ff_sorted, wts_s, g2b[1], lng, lnb, 1, n_smp)

    n_pg_p = t // PAGE
    return (y_p.reshape(n_seq, t, D), y_s.reshape(n_smp, 1, D),
            s_prompt[None], s_sample[None],
            k_p.reshape(n_seq, n_pg_p, PAGE, H_B, 2 * DH_B), v_p.reshape(n_seq, n_pg_p, PAGE, H_B, DV_B),
            k_s.reshape(n_smp, 1, H_B, 2 * DH_B), v_s.reshape(n_smp, 1, H_B, DV_B))
```

```python
import functools
import math

import jax
import jax.numpy as jnp
from jax import lax
from jax.experimental import pallas as pl
from jax.experimental.pallas import tpu as pltpu

f32 = jnp.float32
bf16 = jnp.bfloat16

D = 1024
H_A, DK_A, DV_A = 4, 128, 256
QK_A, VW_A = H_A * DK_A, H_A * DV_A
GATE_RANK = 16
GATE_TAU = 16.0
H_B, DH_B, DV_B = 8, 64, 128
N_EXPERTS, N_GROUPS, GROUP_SIZE = 16, 4, 4
D_FF = 512
N_CLASSES = N_GROUPS * 6
DEPTH = 2
ALPHA = (2.0 * DEPTH) ** 0.25
LN_EPS = 1e-5
RMS_EPS = 1e-6
NEG_BIG = -1e30
PAGE = 128

LANES = 128
VMEM_LIMIT = 48 * 1024 * 1024
TM = 512
TM_E = 256
TM_PERM = 1024
GLA_C = 64
GLA_SEQS = 8
TQ = 512
TH = TQ // 2
PAGES_PER_STEP = 16
SUB = 8
assert D == SUB * LANES


def _load_tokens(ref, n, rows_per_token):
    return jnp.concatenate([ref[pl.ds(h, n, stride=rows_per_token), :] for h in range(SUB)], axis=1)


def _store_tokens(ref, val, offset, rows_per_token):
    n = val.shape[0]
    for h in range(SUB):
        ref[pl.ds(offset + h, n, stride=rows_per_token), :] = val[:, h * LANES:(h + 1) * LANES]


def _bf(a):
    return a.astype(bf16)


def _mm(a, b):
    return jnp.dot(a, b, preferred_element_type=f32)


def _mm_nt(a, b):
    return lax.dot_general(a, b, (((1,), (1,)), ((), ())), preferred_element_type=f32)


def _mm_tn(a, b):
    return lax.dot_general(a, b, (((0,), (0,)), ((), ())), preferred_element_type=f32)


def _params(*sem):
    return pltpu.CompilerParams(dimension_semantics=sem, vmem_limit_bytes=VMEM_LIMIT)


def _layer_norm(h, g, b):
    mu = jnp.mean(h, axis=-1, keepdims=True)
    hc = h - mu
    var = jnp.mean(hc * hc, axis=-1, keepdims=True)
    return hc * lax.rsqrt(var + LN_EPS) * g + b


def _rms_norm(o, g):
    ms = jnp.mean(o * o, axis=-1, keepdims=True)
    return o * lax.rsqrt(ms + RMS_EPS) * g


def _silu(x):
    return x / (1.0 + jnp.exp(-x))


def _row_to_col(row):
    n = row.shape[1]
    eye = lax.broadcasted_iota(jnp.int32, (n, n), 0) == lax.broadcasted_iota(jnp.int32, (n, n), 1)
    return jnp.sum(jnp.where(eye, row, 0.0), axis=1, keepdims=True)


def _ada_kernel(c_ref, w_ref, b_ref, o_ref):
    o_ref[...] = _mm(_bf(c_ref[...]), _bf(w_ref[...])) + b_ref[...]


def _ada(c_all, w3, b3, tn=1024):
    n_l, _, n = w3.shape
    m = c_all.shape[0]
    return pl.pallas_call(
        _ada_kernel,
        grid=(n_l, n // tn),
        in_specs=[pl.BlockSpec((m, D), lambda l, j: (0, 0)),
                  pl.BlockSpec((None, D, tn), lambda l, j: (l, 0, j)),
                  pl.BlockSpec((None, 1, tn), lambda l, j: (l, 0, j))],
        out_specs=pl.BlockSpec((None, m, tn), lambda l, j: (l, 0, j)),
        out_shape=jax.ShapeDtypeStruct((n_l, m, n), f32),
        compiler_params=_params("parallel", "parallel"),
        name="ada",
    )(c_all, w3, b3)


def _mod_spec(mod, tiles_per_block):
    return pl.BlockSpec((None, mod.shape[1], D), lambda i, *_: (i // tiles_per_block, 0, 0))


def _row_spec(tm, width):
    return pl.BlockSpec((tm, width), lambda i, *_: (i, 0))


def _const_spec(shape):
    nd = len(shape)
    return pl.BlockSpec(shape, lambda i, *_: (0,) * nd)


def _proj_a_kernel(x_ref, sh_ref, sc_ref, w_ref, wgd_ref, wgu_ref, bg_ref,
                   q_ref, k_ref, v_ref, r_ref, g_ref):
    xm = _bf(x_ref[...] * (1.0 + sc_ref[...]) + sh_ref[...])
    q_ref[...] = _mm(xm, w_ref[:, 0:QK_A]) * (DK_A ** -0.5)
    k_ref[...] = _mm(xm, w_ref[:, QK_A:2 * QK_A])
    v_ref[...] = _mm(xm, w_ref[:, 2 * QK_A:2 * QK_A + VW_A]).astype(v_ref.dtype)
    r_ref[...] = _mm(xm, w_ref[:, 2 * QK_A + VW_A:2 * QK_A + 2 * VW_A])
    gd = _mm(xm, wgd_ref[...])
    z = _mm(_bf(gd), wgu_ref[...]) + bg_ref[...]
    g_ref[...] = (jnp.minimum(z, 0.0) - jnp.log1p(jnp.exp(-jnp.abs(z)))) * (1.0 / GATE_TAU)


def _proj_a(x, sh, sc, tiles_per_block, tm, w_main, w_gd, w_gu, b_g, v_dtype):
    n = x.shape[0]
    outs = [jax.ShapeDtypeStruct((n, QK_A), f32), jax.ShapeDtypeStruct((n, QK_A), f32),
            jax.ShapeDtypeStruct((n, VW_A), v_dtype), jax.ShapeDtypeStruct((n, VW_A), f32),
            jax.ShapeDtypeStruct((n, QK_A), f32)]
    return pl.pallas_call(
        _proj_a_kernel,
        grid=(n // tm,),
        in_specs=[_row_spec(tm, D), _mod_spec(sh, tiles_per_block), _mod_spec(sc, tiles_per_block),
                  _const_spec(w_main.shape), _const_spec(w_gd.shape), _const_spec(w_gu.shape),
                  _const_spec(b_g.shape)],
        out_specs=[_row_spec(tm, QK_A), _row_spec(tm, QK_A), _row_spec(tm, VW_A),
                   _row_spec(tm, VW_A), _row_spec(tm, QK_A)],
        out_shape=outs,
        compiler_params=_params("parallel"),
        name="proj_a",
    )(x, sh, sc, w_main, w_gd, w_gu, b_g)


def _gla_chunk_kernel(q_ref, k_ref, v_ref, r_ref, g_ref, gn_ref, o_ref, s_ref, st_scr, *, n_chunks):
    c = pl.program_id(1)
    C = q_ref.shape[1]

    @pl.when(c == 0)
    def _():
        st_scr[...] = jnp.zeros_like(st_scr)

    row = lax.broadcasted_iota(jnp.int32, (C, C), 0)
    col = lax.broadcasted_iota(jnp.int32, (C, C), 1)
    tril = col <= row
    tril_b = jnp.where(tril, 1.0, 0.0).astype(bf16)
    mid = C // 2 - 1
    for b in range(q_ref.shape[0]):
        g = g_ref[b]
        g1 = _bf(g)
        rem = g - g1.astype(f32)
        g2 = _bf(rem)
        g3 = _bf(rem - g2.astype(f32))
        cum = _mm(tril_b, g1) + _mm(tril_b, g2) + _mm(tril_b, g3)
        for h in range(H_A):
            ks = slice(h * DK_A, (h + 1) * DK_A)
            vs = slice(h * DV_A, (h + 1) * DV_A)
            cum_h = cum[:, ks]
            m = cum_h[mid:mid + 1, :]
            last = cum_h[C - 1:C, :]
            qh = q_ref[b, :, ks]
            kh = k_ref[b, :, ks]
            vb = v_ref[b, :, vs]
            q_in = _bf(qh * jnp.exp(cum_h))
            q_mid = _bf(qh * jnp.exp(cum_h - m))
            k_mid = _bf(kh * jnp.exp(m - cum_h))
            k_last = _bf(kh * jnp.exp(last - cum_h))
            st = st_scr[b, h]
            inter = _mm_nt(q_in, _bf(st))
            attn = jnp.where(tril, _mm_nt(q_mid, k_mid), 0.0)
            o = inter + _mm(_bf(attn), vb)
            st_new = st * jnp.exp(last) + _mm_tn(vb, k_last)
            st_scr[b, h] = st_new
            rh = r_ref[b, :, vs]
            o_ref[b, :, vs] = _bf(_rms_norm(o, gn_ref[...]) * _silu(rh))

    @pl.when(c == n_chunks - 1)
    def _():
        for b in range(q_ref.shape[0]):
            for h in range(H_A):
                s_ref[b, h] = st_scr[b, h].T


def _gla_chunk(q, k, v, r, g, gn, n_seq, t):
    n_chunks = t // GLA_C
    nb = GLA_SEQS
    rows = lambda w: pl.BlockSpec((nb, GLA_C, w), lambda b, c: (b, c, 0))
    seq = lambda a: a.reshape(n_seq, t, a.shape[1])
    o, s_fin = pl.pallas_call(
        functools.partial(_gla_chunk_kernel, n_chunks=n_chunks),
        grid=(n_seq // nb, n_chunks),
        in_specs=[rows(QK_A), rows(QK_A), rows(VW_A), rows(VW_A), rows(QK_A),
                  pl.BlockSpec((1, DV_A), lambda b, c: (0, 0))],
        out_specs=[rows(VW_A), pl.BlockSpec((nb, H_A, DK_A, DV_A), lambda b, c: (b, 0, 0, 0))],
        out_shape=[jax.ShapeDtypeStruct((n_seq, t, VW_A), bf16),
                   jax.ShapeDtypeStruct((n_seq, H_A, DK_A, DV_A), f32)],
        scratch_shapes=[pltpu.VMEM((nb, H_A, DV_A, DK_A), f32)],
        compiler_params=_params("parallel", "arbitrary"),
        name="gla_chunk",
    )(seq(q), seq(k), seq(v), seq(r), seq(g), gn)
    return o.reshape(n_seq * t, VW_A), s_fin


def _gla_step_kernel(q_ref, k_ref, v_ref, r_ref, g_ref, gn_ref, s_ref, o_ref, sn_ref):
    for h in range(H_A):
        ks = slice(h * DK_A, (h + 1) * DK_A)
        vs = slice(h * DV_A, (h + 1) * DV_A)
        q_col = _row_to_col(q_ref[:, ks])
        k_col = _row_to_col(k_ref[:, ks])
        eg_col = _row_to_col(jnp.exp(g_ref[:, ks]))
        s_new = eg_col * s_ref[h] + k_col * v_ref[:, vs]
        sn_ref[h] = s_new
        o = jnp.sum(q_col * s_new, axis=0, keepdims=True)
        rh = r_ref[:, vs]
        o_ref[:, vs] = _rms_norm(o, gn_ref[...]) * _silu(rh)


def _gla_step(q, k, v, r, g, gn, state):
    n = q.shape[0]
    vec = lambda w: pl.BlockSpec((None, 1, w), lambda b: (b, 0, 0))
    st = pl.BlockSpec((None, H_A, DK_A, DV_A), lambda b: (b, 0, 0, 0))
    o, s_new = pl.pallas_call(
        _gla_step_kernel,
        grid=(n,),
        in_specs=[vec(QK_A), vec(QK_A), vec(VW_A), vec(VW_A), vec(QK_A),
                  pl.BlockSpec((1, DV_A), lambda b: (0, 0)), st],
        out_specs=[vec(VW_A), st],
        out_shape=[jax.ShapeDtypeStruct((n, 1, VW_A), f32),
                   jax.ShapeDtypeStruct((n, H_A, DK_A, DV_A), f32)],
        compiler_params=_params("parallel"),
        name="gla_step",
    )(q.reshape(n, 1, QK_A), k.reshape(n, 1, QK_A), v.reshape(n, 1, VW_A), r.reshape(n, 1, VW_A),
      g.reshape(n, 1, QK_A), gn, state)
    return o.reshape(n, VW_A), s_new


def _route(probs):
    p = [probs[e:e + 1, :] for e in range(N_EXPERTS)]
    scores = []
    for gi in range(N_GROUPS):
        a0, a1, a2, a3 = p[4 * gi:4 * gi + 4]
        hi1, lo1 = jnp.maximum(a0, a1), jnp.minimum(a0, a1)
        hi2, lo2 = jnp.maximum(a2, a3), jnp.minimum(a2, a3)
        scores.append(jnp.maximum(hi1, hi2) + jnp.maximum(jnp.minimum(hi1, hi2), jnp.maximum(lo1, lo2)))
    best = scores[0]
    gidx = jnp.zeros(best.shape, jnp.int32)
    for gi in range(1, N_GROUPS):
        upd = scores[gi] > best
        best = jnp.where(upd, scores[gi], best)
        gidx = jnp.where(upd, gi, gidx)
    a = [jnp.where(gidx == 0, p[j], jnp.where(gidx == 1, p[4 + j], jnp.where(gidx == 2, p[8 + j], p[12 + j])))
         for j in range(GROUP_SIZE)]
    v1 = a[0]
    i1 = jnp.zeros(best.shape, jnp.int32)
    for j in range(1, GROUP_SIZE):
        upd = a[j] > v1
        v1 = jnp.where(upd, a[j], v1)
        i1 = jnp.where(upd, j, i1)
    v2 = jnp.full(best.shape, -1.0, f32)
    i2 = jnp.zeros(best.shape, jnp.int32)
    for j in range(GROUP_SIZE):
        upd = jnp.logical_and(i1 != j, a[j] > v2)
        v2 = jnp.where(upd, a[j], v2)
        i2 = jnp.where(upd, j, i2)
    tot = v1 + v2
    w1 = v1 / tot
    w2 = v2 / tot
    first_low = i1 < i2
    lo = jnp.minimum(i1, i2)
    hi = jnp.maximum(i1, i2)
    pair = lo * 3 - ((lo * (lo - 1)) >> 1) + hi - lo - 1
    cls = gidx * 6 + pair
    return cls, jnp.where(first_low, w1, w2), jnp.where(first_low, w2, w1)


def _post_kernel(a_ref, w_ref, x_ref, g1_ref, lng_ref, lnb_ref, sh2_ref, sc2_ref, wr_ref, br_ref,
                 x1_ref, rows_ref, wts_ref, meta_ref, cnt_ref, cnt_scr):
    i = pl.program_id(0)
    tm = x_ref.shape[0]

    @pl.when(i == 0)
    def _():
        cnt_scr[...] = jnp.zeros_like(cnt_scr)

    mix = _mm(_bf(a_ref[...]), w_ref[...])
    x1 = _layer_norm(ALPHA * x_ref[...] + g1_ref[...] * mix, lng_ref[...], lnb_ref[...])
    x1_ref[...] = x1
    xm = x1 * (1.0 + sc2_ref[...]) + sh2_ref[...]
    _store_tokens(rows_ref, xm, 0, SUB)

    x_hi = _bf(xm)
    x_lo = _bf(xm - x_hi.astype(f32))
    wr = wr_ref[...]
    l_hi = _mm_nt(wr, x_hi)
    l_lo = _mm_nt(wr[0:N_EXPERTS], x_lo)
    logits = l_hi[0:N_EXPERTS] + l_hi[N_EXPERTS:2 * N_EXPERTS] + l_lo + br_ref[...]
    mx = jnp.max(logits, axis=0, keepdims=True)
    ex = jnp.exp(logits - mx)
    probs = ex / jnp.sum(ex, axis=0, keepdims=True)
    cls, w_lo, w_hi = _route(probs)

    crow = lax.broadcasted_iota(jnp.int32, (32, tm), 0)
    onehot = crow == cls
    oh = jnp.where(onehot, 1.0, 0.0)
    before = lax.broadcasted_iota(jnp.int32, (tm, tm), 0) < lax.broadcasted_iota(jnp.int32, (tm, tm), 1)
    prefix = _mm(_bf(oh), jnp.where(before, 1.0, 0.0).astype(bf16))
    carry = cnt_scr[...]
    rank = jnp.sum(jnp.where(onehot, prefix + carry, 0.0), axis=0, keepdims=True)
    cnt_new = carry + jnp.sum(oh, axis=1, keepdims=True)
    cnt_scr[...] = cnt_new
    cnt_ref[...] = jnp.broadcast_to(cnt_new, cnt_ref.shape)

    mrow = lax.broadcasted_iota(jnp.int32, (8, tm), 0)
    meta_ref[...] = jnp.where(mrow == 0, cls, jnp.where(mrow == 1, rank.astype(jnp.int32), 0))

    lane = lax.broadcasted_iota(jnp.int32, (tm, LANES), 1)
    wts_ref[...] = jnp.where(lane == 0, _row_to_col(w_lo), jnp.where(lane == 1, _row_to_col(w_hi), 0.0))


def _post(a, w_o, x, g1, lng, lnb, sh2, sc2, tiles_per_block, tm, wr_t, br_col):
    n = x.shape[0]
    mod = lambda arr: _mod_spec(arr, tiles_per_block)
    return pl.pallas_call(
        _post_kernel,
        grid=(n // tm,),
        in_specs=[_row_spec(tm, D), _const_spec(w_o.shape), _row_spec(tm, D), mod(g1),
                  _const_spec(lng.shape), _const_spec(lnb.shape), mod(sh2), mod(sc2),
                  _const_spec(wr_t.shape), _const_spec(br_col.shape)],
        out_specs=[_row_spec(tm, D), _row_spec(tm * SUB, LANES), _row_spec(tm, LANES),
                   pl.BlockSpec((8, tm), lambda i: (0, i)),
                   pl.BlockSpec((32, LANES), lambda i: (0, 0))],
        out_shape=[jax.ShapeDtypeStruct((n, D), f32), jax.ShapeDtypeStruct((n * SUB, LANES), f32),
                   jax.ShapeDtypeStruct((n, LANES), f32), jax.ShapeDtypeStruct((8, n), jnp.int32),
                   jax.ShapeDtypeStruct((32, LANES), f32)],
        scratch_shapes=[pltpu.VMEM((32, 1), f32)],
        compiler_params=_params("arbitrary"),
        name="post",
    )(a, w_o, x, g1, lng, lnb, sh2, sc2, wr_t, br_col)


def _scatter_kernel(pos_ref, src_ref, dst_in_ref, dst_ref, sem, *, tm, rows):
    del dst_in_ref
    base = pl.program_id(0) * tm

    def token_copy(r):
        slot = pl.multiple_of(pos_ref[base + r] * rows, rows)
        return pltpu.make_async_copy(src_ref.at[pl.ds(pl.multiple_of(r * rows, rows), rows)],
                                     dst_ref.at[pl.ds(slot, rows)], sem)

    def start(r, carry):
        token_copy(r).start()
        return carry

    def wait(r, carry):
        token_copy(r).wait()
        return carry

    lax.fori_loop(0, tm, start, 0, unroll=8)
    lax.fori_loop(0, tm, wait, 0, unroll=8)


def _scatter_rows(pos, src, dst, rows):
    n = pos.shape[0]
    tm = min(TM_PERM, n)
    return pl.pallas_call(
        functools.partial(_scatter_kernel, tm=tm, rows=rows),
        grid_spec=pltpu.PrefetchScalarGridSpec(
            num_scalar_prefetch=1, grid=(n // tm,),
            in_specs=[pl.BlockSpec((tm * rows, LANES), lambda t, pos: (t, 0)),
                      pl.BlockSpec(memory_space=pl.ANY)],
            out_specs=pl.BlockSpec(memory_space=pl.ANY),
            scratch_shapes=[pltpu.SemaphoreType.DMA(())]),
        out_shape=jax.ShapeDtypeStruct(dst.shape, dst.dtype),
        input_output_aliases={2: 0},
        compiler_params=_params("arbitrary"),
        name="scatter_rows",
    )(pos, src, dst)


def _gather_tokens(pos_ref, src_ref, buf_ref, sem, tile, tm, rows, wait):
    base = tile * tm

    def token_copy(r):
        slot = pl.multiple_of(pos_ref[base + r] * rows, rows)
        return pltpu.make_async_copy(src_ref.at[pl.ds(slot, rows)],
                                     buf_ref.at[pl.ds(pl.multiple_of(r * rows, rows), rows)], sem)

    def step(r, carry):
        if wait:
            token_copy(r).wait()
        else:
            token_copy(r).start()
        return carry

    lax.fori_loop(0, tm, step, 0, unroll=8)


def _prefetched_tokens(pos_ref, src_ref, buf_ref, sem_ref, tm, rows):
    i = pl.program_id(0)
    slot = i % 2

    @pl.when(i == 0)
    def _():
        _gather_tokens(pos_ref, src_ref, buf_ref.at[0], sem_ref.at[0], 0, tm, rows, wait=False)

    @pl.when(i + 1 < pl.num_programs(0))
    def _():
        _gather_tokens(pos_ref, src_ref, buf_ref.at[1 - slot], sem_ref.at[1 - slot], i + 1, tm, rows, wait=False)

    _gather_tokens(pos_ref, src_ref, buf_ref.at[slot], sem_ref.at[slot], i, tm, rows, wait=True)
    return buf_ref.at[slot]


def _experts_kernel(elo_ref, ehi_ref, valid_ref, rows_ref,
                    wg_lo, wu_lo, wd_lo, wg_hi, wu_hi, wd_hi, o_ref):
    t = pl.program_id(0)

    @pl.when(valid_ref[t] == 1)
    def _():
        xb = _bf(_load_tokens(rows_ref, TM_E, SUB))

        def ffn(wg, wu, wd):
            hid = _silu(_mm(xb, wg[...])) * _mm(xb, wu[...])
            return _mm(_bf(hid), wd[...])

        _store_tokens(o_ref, ffn(wg_lo, wu_lo, wd_lo), 0, 2 * SUB)
        _store_tokens(o_ref, ffn(wg_hi, wu_hi, wd_hi), SUB, 2 * SUB)

    @pl.when(valid_ref[t] == 0)
    def _():
        o_ref[...] = jnp.zeros_like(o_ref)


def _experts(e_lo, e_hi, valid, rows, wg, wu, wd, layer):
    n_tiles = rows.shape[0] // (TM_E * SUB)
    up = lambda sel: pl.BlockSpec((None, None, D, D_FF), lambda t, lo, hi, v: (layer, sel(lo, hi)[t], 0, 0))
    down = lambda sel: pl.BlockSpec((None, None, D_FF, D), lambda t, lo, hi, v: (layer, sel(lo, hi)[t], 0, 0))
    first = lambda lo, hi: lo
    second = lambda lo, hi: hi
    return pl.pallas_call(
        _experts_kernel,
        grid_spec=pltpu.PrefetchScalarGridSpec(
            num_scalar_prefetch=3, grid=(n_tiles,),
            in_specs=[pl.BlockSpec((TM_E * SUB, LANES), lambda t, lo, hi, v: (t, 0)),
                      up(first), up(first), down(first), up(second), up(second), down(second)],
            out_specs=pl.BlockSpec((TM_E * 2 * SUB, LANES), lambda t, lo, hi, v: (t, 0))),
        out_shape=jax.ShapeDtypeStruct((2 * rows.shape[0], LANES), f32),
        compiler_params=_params("parallel"),
        name="experts",
    )(e_lo, e_hi, valid, rows, wg, wu, wd, wg, wu, wd)


def _moe(rows_p, meta_p, cnt_p, rows_s, meta_s, cnt_s, wg, wu, wd, layer):
    n_p, n_s = meta_p.shape[1], meta_s.shape[1]
    n_tiles = -(-(n_p + n_s) // TM_E) + N_CLASSES
    cp = cnt_p[:N_CLASSES, 0].astype(jnp.int32)
    cs = cnt_s[:N_CLASSES, 0].astype(jnp.int32)
    tiles_c = (cp + cs + TM_E - 1) // TM_E
    tile_end = jnp.cumsum(tiles_c)
    row_off = (tile_end - tiles_c) * TM_E
    pos_p = row_off[meta_p[0]] + meta_p[1]
    pos_s = row_off[meta_s[0]] + cp[meta_s[0]] + meta_s[1]
    t_idx = jnp.arange(n_tiles, dtype=jnp.int32)
    valid = (t_idx < tile_end[-1]).astype(jnp.int32)
    tile_cls = jnp.minimum(jnp.sum((tile_end[None, :] <= t_idx[:, None]).astype(jnp.int32), axis=1), N_CLASSES - 1)
    pairs = [(a, b) for a in range(GROUP_SIZE) for b in range(a + 1, GROUP_SIZE)]
    lo_tab = jnp.asarray([gi * GROUP_SIZE + a for gi in range(N_GROUPS) for a, _ in pairs], jnp.int32)
    hi_tab = jnp.asarray([gi * GROUP_SIZE + b for gi in range(N_GROUPS) for _, b in pairs], jnp.int32)
    e_lo, e_hi = lo_tab[tile_cls], hi_tab[tile_cls]

    buf = jnp.zeros((n_tiles * TM_E * SUB, LANES), f32)
    buf = _scatter_rows(pos_p, rows_p, buf, SUB)
    buf = _scatter_rows(pos_s, rows_s, buf, SUB)
    ff_sorted = _experts(e_lo, e_hi, valid, buf, wg, wu, wd, layer)
    return ff_sorted, pos_p, pos_s


def _ffn_out(pos_ref, sorted_ref, wts_ref, buf_ref, sem_ref, n):
    pair_ref = _prefetched_tokens(pos_ref, sorted_ref, buf_ref, sem_ref, n, 2 * SUB)
    wts = wts_ref[...]
    lo = jnp.concatenate([pair_ref[pl.ds(h, n, stride=2 * SUB), :] for h in range(SUB)], axis=1)
    hi = jnp.concatenate([pair_ref[pl.ds(SUB + h, n, stride=2 * SUB), :] for h in range(SUB)], axis=1)
    return wts[:, 0:1] * lo + wts[:, 1:2] * hi


def _mid_kernel(pos_ref, x1_ref, sorted_ref, wts_ref, g2_ref, lng_ref, lnb_ref, shkv_ref, sckv_ref, shq_ref,
                scq_ref, wkv_ref, wq_ref, x2_ref, k_ref, v_ref, kb_ref, vb_ref, q_ref, buf_ref, sem_ref):
    ff = _ffn_out(pos_ref, sorted_ref, wts_ref, buf_ref, sem_ref, x1_ref.shape[0])
    x2 = _layer_norm(ALPHA * x1_ref[...] + g2_ref[...] * ff, lng_ref[...], lnb_ref[...])
    x2_ref[...] = x2
    xkv = _bf(x2 * (1.0 + sckv_ref[...]) + shkv_ref[...])
    kk = _mm(xkv, wkv_ref[:, 0:D])
    vv = _mm(xkv, wkv_ref[:, D:2 * D])
    _store_tokens(k_ref, kk, 0, SUB)
    _store_tokens(v_ref, vv, 0, SUB)
    kb_ref[...] = _bf(kk)
    vb_ref[...] = _bf(vv)
    xq = _bf(x2 * (1.0 + scq_ref[...]) + shq_ref[...])
    q_ref[...] = _bf(_mm(xq, wq_ref[...]) * (DH_B ** -0.5))


def _pair_scratch(tm):
    return [pltpu.VMEM((2, tm * 2 * SUB, LANES), f32), pltpu.SemaphoreType.DMA((2,))]


def _mid(pos, x1, ff_sorted, wts, g2, lng, lnb, shkv, sckv, shq, scq, tiles_per_block, tm, w_kv, w_q):
    n = x1.shape[0]
    mod = lambda arr: _mod_spec(arr, tiles_per_block)
    sds = lambda dt: jax.ShapeDtypeStruct((n, D), dt)
    kv_sds = jax.ShapeDtypeStruct((n * SUB, LANES), f32)
    return pl.pallas_call(
        _mid_kernel,
        grid_spec=pltpu.PrefetchScalarGridSpec(
            num_scalar_prefetch=1, grid=(n // tm,),
            in_specs=[_row_spec(tm, D), pl.BlockSpec(memory_space=pl.ANY), _row_spec(tm, LANES), mod(g2),
                      _const_spec(lng.shape), _const_spec(lnb.shape),
                      mod(shkv), mod(sckv), mod(shq), mod(scq), _const_spec(w_kv.shape), _const_spec(w_q.shape)],
            out_specs=[_row_spec(tm, D), _row_spec(tm * SUB, LANES), _row_spec(tm * SUB, LANES),
                       _row_spec(tm, D), _row_spec(tm, D), _row_spec(tm, D)],
            scratch_shapes=_pair_scratch(tm)),
        out_shape=[sds(f32), kv_sds, kv_sds, sds(bf16), sds(bf16), sds(bf16)],
        compiler_params=_params("arbitrary"),
        name="mid",
    )(pos, x1, ff_sorted, wts, g2, lng, lnb, shkv, sckv, shq, scq, w_kv, w_q)


def _lambda(lq1_ref, lk1_ref, lq2_ref, lk2_ref, lam_init):
    s1 = jnp.sum(lq1_ref[...] * lk1_ref[...], axis=1, keepdims=True)
    s2 = jnp.sum(lq2_ref[...] * lk2_ref[...], axis=1, keepdims=True)
    return jnp.exp(s1) - jnp.exp(s2) + lam_init


def _online_update(s, v, m_ref, l_ref, acc_ref):
    m_old = m_ref[...]
    m_new = jnp.maximum(m_old, jnp.max(s, axis=1, keepdims=True))
    p = jnp.exp(s - jnp.tile(m_new, (1, s.shape[1] // LANES)))
    corr = jnp.exp(m_old - m_new)
    l_ref[...] = l_ref[...] * corr + jnp.sum(p, axis=1, keepdims=True)
    acc_ref[...] = acc_ref[...] * jnp.tile(corr, (1, acc_ref.shape[1] // LANES)) + _mm(p.astype(v.dtype), v)
    m_ref[...] = m_new


def _dattn_kernel(slope_ref, q_ref, k_ref, v_ref, lq1_ref, lk1_ref, lq2_ref, lk2_ref, subln_ref,
                  o_ref, m_scr, l_scr, acc_scr, *, lam_init):
    t = q_ref.shape[0]
    slope = slope_ref[pl.program_id(1)]
    lam = _lambda(lq1_ref, lk1_ref, lq2_ref, lk2_ref, lam_init)
    first_map = lax.broadcasted_iota(jnp.int32, (1, 2 * DH_B), 1) < DH_B
    row = lax.broadcasted_iota(jnp.int32, (2 * TQ, 1), 0)
    rel_q = (row & (TH - 1)) + ((row >> 9) << 8)
    assert TH == 256 and TQ == 512

    def attend(qs, rel_rows, rows, q0, k0, nk, masked):
        s = _mm_nt(qs, k_ref[pl.ds(k0, nk), :])
        rel_k = lax.broadcasted_iota(jnp.int32, (1, nk), 1) + (k0 - q0)
        s = s + slope * rel_k.astype(f32)
        if masked:
            s = jnp.where(rel_k <= rel_rows, s, NEG_BIG)
        _online_update(s, v_ref[pl.ds(k0, nk), :], m_scr.at[rows], l_scr.at[rows], acc_scr.at[rows])

    for i in range(t // TQ):
        q0 = i * TQ
        m_scr[...] = jnp.full(m_scr.shape, NEG_BIG, f32)
        l_scr[...] = jnp.zeros_like(l_scr)
        acc_scr[...] = jnp.zeros_like(acc_scr)
        blocks = []
        for half in range(2):
            q = q_ref[q0 + half * TH:q0 + (half + 1) * TH, :]
            zero = jnp.zeros_like(q)
            blocks += [jnp.where(first_map, q, zero), jnp.where(first_map, zero, q)]
        qs = jnp.concatenate(blocks, axis=0)
        every = pl.ds(0, 2 * TQ)
        second_half = pl.ds(TQ, TQ)

        def below_diagonal(j, carry):
            attend(qs, rel_q, every, q0, pl.multiple_of(j * TQ, TQ), TQ, False)
            return carry

        lax.fori_loop(0, i, below_diagonal, 0)
        attend(qs, rel_q, every, q0, q0, TH, True)
        attend(qs[TQ:2 * TQ], rel_q[TQ:2 * TQ], second_half, q0, q0 + TH, TH, True)

        for half in range(2):
            r0, r1 = 2 * half * TH, (2 * half + 1) * TH
            o1 = acc_scr[r0:r0 + TH] / l_scr[r0:r0 + TH]
            o2 = acc_scr[r1:r1 + TH] / l_scr[r1:r1 + TH]
            o = o1 - lam * o2
            o_ref[q0 + half * TH:q0 + (half + 1) * TH, :] = _bf(_rms_norm(o, subln_ref[...]) * (1.0 - lam_init))


def _dattn(slopes, q, k, v, lq1, lk1, lq2, lk2, subln, n_seq, t, lam_init):
    head = pl.BlockSpec((t, 2 * DH_B), lambda b, h: (b, h))
    small = lambda w: pl.BlockSpec((1, w), lambda b, h: (0, 0))
    return pl.pallas_call(
        functools.partial(_dattn_kernel, lam_init=lam_init),
        grid=(n_seq, H_B),
        in_specs=[pl.BlockSpec(memory_space=pltpu.SMEM), head, head, head,
                  small(DH_B), small(DH_B), small(DH_B), small(DH_B), small(DV_B)],
        out_specs=head,
        out_shape=jax.ShapeDtypeStruct((n_seq * t, H_B * DV_B), bf16),
        scratch_shapes=[pltpu.VMEM((2 * TQ, LANES), f32), pltpu.VMEM((2 * TQ, LANES), f32),
                        pltpu.VMEM((2 * TQ, DV_B), f32)],
        compiler_params=_params("parallel", "parallel"),
        name="dattn_prompt",
    )(slopes, q, k, v, lq1, lk1, lq2, lk2, subln)


def _dattn_step_kernel(pt_ref, q_ref, kn_ref, vn_ref, nslope_ref, lq1_ref, lk1_ref, lq2_ref, lk2_ref,
                       subln_ref, *rest, past_len, lam_init):
    n_pg = PAGES_PER_STEP
    k_refs, v_refs = rest[0:n_pg], rest[n_pg:2 * n_pg]
    o_ref, qbd_scr, m_scr, l_scr, acc_scr = rest[2 * n_pg:]
    j = pl.program_id(1)
    rows = 2 * H_B

    @pl.when(j == 0)
    def _():
        lane = lax.broadcasted_iota(jnp.int32, (rows, D), 1)
        rowi = lax.broadcasted_iota(jnp.int32, (rows, D), 0)
        q = jnp.broadcast_to(q_ref[...].astype(f32), (rows, D))
        qbd_scr[...] = jnp.where((lane >> 6) == rowi, q, 0.0)
        m_scr[...] = jnp.full(m_scr.shape, NEG_BIG, f32)
        l_scr[...] = jnp.zeros_like(l_scr)
        acc_scr[...] = jnp.zeros_like(acc_scr)

    qbd = qbd_scr[...]
    s = jnp.concatenate([_mm_nt(qbd, _load_tokens(k_refs[u], PAGE, SUB)) for u in range(n_pg)], axis=1)
    kpos = j * (n_pg * PAGE) + lax.broadcasted_iota(jnp.int32, (1, n_pg * PAGE), 1)
    s = s + nslope_ref[...] * (past_len - kpos).astype(f32)
    m_old = m_scr[...]
    m_new = jnp.maximum(m_old, jnp.max(s, axis=1, keepdims=True))
    p = jnp.exp(s - jnp.tile(m_new, (1, n_pg)))
    corr = jnp.exp(m_old - m_new)
    l_scr[...] = l_scr[...] * corr + jnp.sum(p, axis=1, keepdims=True)
    pv = _mm(p[:, 0:PAGE], _load_tokens(v_refs[0], PAGE, SUB))
    for u in range(1, n_pg):
        pv = pv + _mm(p[:, u * PAGE:(u + 1) * PAGE], _load_tokens(v_refs[u], PAGE, SUB))
    acc_scr[...] = acc_scr[...] * jnp.tile(corr, (1, SUB)) + pv
    m_scr[...] = m_new

    @pl.when(j == pl.num_programs(1) - 1)
    def _():
        kn = jnp.concatenate([kn_ref[h:h + 1, :] for h in range(H_B)], axis=1)
        vn = jnp.concatenate([vn_ref[h:h + 1, :] for h in range(H_B)], axis=1)
        kn_b = _bf(kn).astype(f32)
        vn_b = _bf(vn).astype(f32)
        s = jnp.sum(qbd * kn_b, axis=1, keepdims=True)
        m_old = m_scr[...]
        m_new = jnp.maximum(m_old, s)
        p = jnp.exp(s - m_new)
        corr = jnp.exp(m_old - m_new)
        l_fin = l_scr[...] * corr + p
        acc = acc_scr[...] * jnp.tile(corr, (1, SUB)) + _bf(jnp.tile(p, (1, SUB))).astype(f32) * vn_b
        lane = lax.broadcasted_iota(jnp.int32, (rows, D), 1)
        rowi = lax.broadcasted_iota(jnp.int32, (rows, D), 0)
        a = jnp.where((lane >> 7) == (rowi >> 1), acc / jnp.tile(l_fin, (1, SUB)), 0.0)
        a1 = jnp.sum(jnp.where((rowi & 1) == 0, a, 0.0), axis=0, keepdims=True)
        a2 = jnp.sum(jnp.where((rowi & 1) == 1, a, 0.0), axis=0, keepdims=True)
        o = a1 - _lambda(lq1_ref, lk1_ref, lq2_ref, lk2_ref, lam_init) * a2
        for h in range(H_B):
            hs = slice(h * DV_B, (h + 1) * DV_B)
            o_ref[:, hs] = _rms_norm(o[:, hs], subln_ref[...]) * (1.0 - lam_init)


def _dattn_step(page_table, q, k_new, v_new, cache_k, cache_v, nslope_rows, lq1, lk1, lq2, lk2, subln,
                past_len, lam_init):
    n, n_pages = page_table.shape
    n_pool = cache_k.shape[0]
    ck = cache_k.reshape(n_pool, PAGE * H_B, DV_B)
    cv = cache_v.reshape(n_pool, PAGE * H_B, DV_B)
    n_pg = PAGES_PER_STEP
    vec = pl.BlockSpec((None, 1, D), lambda b, j, pt: (b, 0, 0))
    new_kv = pl.BlockSpec((SUB, LANES), lambda b, j, pt: (b, 0))
    const = lambda shape: pl.BlockSpec(shape, lambda b, j, pt: (0,) * len(shape))

    def page(u):
        return pl.BlockSpec((None, PAGE * H_B, DV_B), lambda b, j, pt: (pt[b, j * n_pg + u], 0, 0))

    pages = [page(u) for u in range(n_pg)]
    out = pl.pallas_call(
        functools.partial(_dattn_step_kernel, past_len=past_len, lam_init=lam_init),
        grid_spec=pltpu.PrefetchScalarGridSpec(
            num_scalar_prefetch=1, grid=(n, n_pages // n_pg),
            in_specs=[vec, new_kv, new_kv, const((2 * H_B, 1)), const((1, DH_B)), const((1, DH_B)),
                      const((1, DH_B)), const((1, DH_B)), const((1, DV_B))] + pages + pages,
            out_specs=vec,
            scratch_shapes=[pltpu.VMEM((2 * H_B, D), f32), pltpu.VMEM((2 * H_B, LANES), f32),
                            pltpu.VMEM((2 * H_B, LANES), f32), pltpu.VMEM((2 * H_B, D), f32)]),
        out_shape=jax.ShapeDtypeStruct((n, 1, D), f32),
        compiler_params=_params("parallel", "arbitrary"),
        name="dattn_step",
    )(page_table, q.reshape(n, 1, D), k_new, v_new, nslope_rows,
      lq1, lk1, lq2, lk2, subln, *([ck] * n_pg), *([cv] * n_pg))
    return out.reshape(n, D)


def _final_kernel(pos_ref, x_ref, sorted_ref, wts_ref, g2_ref, lng_ref, lnb_ref, y_ref, buf_ref, sem_ref):
    ff = _ffn_out(pos_ref, sorted_ref, wts_ref, buf_ref, sem_ref, x_ref.shape[0])
    y_ref[...] = _layer_norm(ALPHA * x_ref[...] + g2_ref[...] * ff, lng_ref[...], lnb_ref[...])


def _final(pos, x, ff_sorted, wts, g2, lng, lnb, tiles_per_block, tm):
    n = x.shape[0]
    return pl.pallas_call(
        _final_kernel,
        grid_spec=pltpu.PrefetchScalarGridSpec(
            num_scalar_prefetch=1, grid=(n // tm,),
            in_specs=[_row_spec(tm, D), pl.BlockSpec(memory_space=pl.ANY), _row_spec(tm, LANES),
                      _mod_spec(g2, tiles_per_block), _const_spec(lng.shape), _const_spec(lnb.shape)],
            out_specs=_row_spec(tm, D),
            scratch_shapes=_pair_scratch(tm)),
        out_shape=jax.ShapeDtypeStruct((n, D), f32),
        compiler_params=_params("arbitrary"),
        name="final",
    )(pos, x, ff_sorted, wts, g2, lng, lnb)


def kernel(x_prompt, x_sample, c_prompt, c_sample, state_gla, cache_k, cache_v, page_table, w_ada, b_ada,
           ln_mix_g, ln_mix_b, ln_ffn_g, ln_ffn_b, w_in_a, w_gate_up_a, b_gate_a, gn_a, w_o_a, w_ada_kv,
           b_ada_kv, w_kv, w_q_b, lam_q1, lam_k1, lam_q2, lam_k2, subln_b, w_o_b, w_router, b_router,
           w_exp_gate, w_exp_up, w_exp_down):
    n_seq, t, _ = x_prompt.shape
    n_smp = x_sample.shape[0]
    n_p = n_seq * t
    past_len = page_table.shape[1] * PAGE
    xp = x_prompt.reshape(n_p, D)
    xs = x_sample.reshape(n_smp, D)
    tpb = t // TM

    c_all = jnp.concatenate([c_prompt, c_sample], axis=0)
    ada = _ada(c_all, w_ada, b_ada.reshape(DEPTH, 1, 6 * D))
    ada_kv = _ada(c_all, w_ada_kv.reshape(1, D, 2 * D), b_ada_kv.reshape(1, 1, 2 * D))[0]

    def mods(table, idx):
        part = table[:, idx * D:(idx + 1) * D]
        return part[:n_seq].reshape(n_seq, 1, D), part[n_seq:].reshape(1, n_smp, D)

    row = lambda a: a.reshape(1, -1)

    w_in = w_in_a[0]
    w_main = _bf(w_in[:, :2 * QK_A + 2 * VW_A])
    w_gd = _bf(jnp.pad(w_in[:, 2 * QK_A + 2 * VW_A:], ((0, 0), (0, LANES - GATE_RANK))))
    w_gu = _bf(jnp.pad(w_gate_up_a[0], ((0, LANES - GATE_RANK), (0, 0))))
    wr_t = w_router.T
    wr_hi = _bf(wr_t)
    wr_split = jnp.concatenate([wr_hi, _bf(wr_t - wr_hi.astype(f32))], axis=0)
    br_col = b_router.reshape(N_EXPERTS, 1)
    wg, wu, wd = _bf(w_exp_gate), _bf(w_exp_up), _bf(w_exp_down)

    sh1, sc1, g1, sh2, sc2, g2 = [mods(ada[0], i) for i in range(6)]
    b_g = row(b_gate_a[0])
    qp, kp, vp, rp, gp = _proj_a(xp, sh1[0], sc1[0], tpb, TM, w_main, w_gd, w_gu, b_g, bf16)
    qs, ks, vs, rs, gs = _proj_a(xs, sh1[1], sc1[1], 1, n_smp, w_main, w_gd, w_gu, b_g, f32)
    gn = row(gn_a[0])
    ap, s_prompt = _gla_chunk(qp, kp, vp, rp, gp, gn, n_seq, t)
    as_, s_sample = _gla_step(qs, ks, vs, rs, gs, gn, state_gla[0])

    w_oa = _bf(w_o_a[0])
    lng, lnb = row(ln_mix_g[0]), row(ln_mix_b[0])
    x1p, rows_p, wts_p, meta_p, cnt_p = _post(ap, w_oa, xp, g1[0], lng, lnb, sh2[0], sc2[0], tpb, TM,
                                              wr_split, br_col)
    x1s, rows_s, wts_s, meta_s, cnt_s = _post(as_, w_oa, xs, g1[1], lng, lnb, sh2[1], sc2[1], 1, n_smp,
                                              wr_split, br_col)
    ff_sorted, pos_p, pos_s = _moe(rows_p, meta_p, cnt_p, rows_s, meta_s, cnt_s, wg, wu, wd, 0)

    shkv, sckv = mods(ada_kv, 0), mods(ada_kv, 1)
    sh1, sc1, g1b, sh2b, sc2b, g2b = [mods(ada[1], i) for i in range(6)]
    lng, lnb = row(ln_ffn_g[0]), row(ln_ffn_b[0])
    w_kvb, w_qb = _bf(w_kv), _bf(w_q_b[0])
    x2p, k_p, v_p, kb_p, vb_p, q_p = _mid(pos_p, x1p, ff_sorted, wts_p, g2[0], lng, lnb, shkv[0], sckv[0],
                                          sh1[0], sc1[0], tpb, TM, w_kvb, w_qb)
    x2s, k_s, v_s, _, _, q_s = _mid(pos_s, x1s, ff_sorted, wts_s, g2[1], lng, lnb, shkv[1], sckv[1],
                                    sh1[1], sc1[1], 1, n_smp, w_kvb, w_qb)

    lam_init = 0.8 - 0.6 * math.exp(-0.3 * 1)
    slopes = jnp.asarray([2.0 ** (-8.0 * (i + 1) / H_B) for i in range(H_B)], f32)
    nslope_rows = -jnp.repeat(slopes, 2).reshape(2 * H_B, 1)
    lam_args = (row(lam_q1[0]), row(lam_k1[0]), row(lam_q2[0]), row(lam_k2[0]), row(subln_b[0]))
    bp = _dattn(slopes, q_p, kb_p, vb_p, *lam_args, n_seq, t, lam_init)
    bs = _dattn_step(page_table, q_s, k_s, v_s, cache_k, cache_v, nslope_rows, *lam_args, past_len, lam_init)

    w_ob = _bf(w_o_b[0])
    lng, lnb = row(ln_mix_g[1]), row(ln_mix_b[1])
    x3p, rows_p, wts_p, meta_p, cnt_p = _post(bp, w_ob, x2p, g1b[0], lng, lnb, sh2b[0], sc2b[0], tpb, TM,
                                              wr_split, br_col)
    x3s, rows_s, wts_s, meta_s, cnt_s = _post(bs, w_ob, x2s, g1b[1], lng, lnb, sh2b[1], sc2b[1], 1, n_smp,
                                              wr_split, br_col)
    ff_sorted, pos_p, pos_s = _moe(rows_p, meta_p, cnt_p, rows_s, meta_s, cnt_s, wg, wu, wd, 1)
    lng, lnb = row(ln_ffn_g[1]), row(ln_ffn_b[1])
    y_p = _final(pos_p, x3p, ff_sorted, wts_p, g2b[0], lng, lnb, tpb, TM)
    y_s = _final(pos_s, x3s, ff_sorted, wts_s, g2b[1], lng, lnb, 1, n_smp)

    n_pg_p = t // PAGE
    return (y_p.reshape(n_seq, t, D), y_s.reshape(n_smp, 1, D),
            s_prompt[None], s_sample[None],
            k_p.reshape(n_seq, n_pg_p, PAGE, H_B, 2 * DH_B), v_p.reshape(n_seq, n_pg_p, PAGE, H_B, DV_B),
            k_s.reshape(n_smp, 1, H_B, 2 * DH_B), v_s.reshape(n_smp, 1, H_B, DV_B))
```

```python
import functools
import math

import jax
import jax.numpy as jnp
from jax import lax
from jax.experimental import pallas as pl
from jax.experimental.pallas import tpu as pltpu

f32 = jnp.float32
bf16 = jnp.bfloat16

D = 1024
H_A, DK_A, DV_A = 4, 128, 256
QK_A, VW_A = H_A * DK_A, H_A * DV_A
GATE_RANK = 16
GATE_TAU = 16.0
H_B, DH_B, DV_B = 8, 64, 128
N_EXPERTS, N_GROUPS, GROUP_SIZE = 16, 4, 4
D_FF = 512
N_CLASSES = N_GROUPS * 6
DEPTH = 2
ALPHA = (2.0 * DEPTH) ** 0.25
LN_EPS = 1e-5
RMS_EPS = 1e-6
NEG_BIG = -1e30
PAGE = 128

LANES = 128
VMEM_LIMIT = 48 * 1024 * 1024
TM = 512
TM_E = 384
TM_PERM = 1024
GLA_C = 64
GLA_SEQS = 8
TQ = 512
TH = TQ // 2
PAGES_PER_STEP = 16
SUB = 8
assert D == SUB * LANES


def _load_tokens(ref, n, rows_per_token):
    return jnp.concatenate([ref[pl.ds(h, n, stride=rows_per_token), :] for h in range(SUB)], axis=1)


def _store_tokens(ref, val, offset, rows_per_token):
    n = val.shape[0]
    for h in range(SUB):
        ref[pl.ds(offset + h, n, stride=rows_per_token), :] = val[:, h * LANES:(h + 1) * LANES]


def _bf(a):
    return a.astype(bf16)


def _mm(a, b):
    return jnp.dot(a, b, preferred_element_type=f32)


def _mm_nt(a, b):
    return lax.dot_general(a, b, (((1,), (1,)), ((), ())), preferred_element_type=f32)


def _mm_tn(a, b):
    return lax.dot_general(a, b, (((0,), (0,)), ((), ())), preferred_element_type=f32)


def _params(*sem):
    return pltpu.CompilerParams(dimension_semantics=sem, vmem_limit_bytes=VMEM_LIMIT)


def _layer_norm(h, g, b):
    mu = jnp.mean(h, axis=-1, keepdims=True)
    hc = h - mu
    var = jnp.mean(hc * hc, axis=-1, keepdims=True)
    return hc * lax.rsqrt(var + LN_EPS) * g + b


def _rms_norm(o, g):
    ms = jnp.mean(o * o, axis=-1, keepdims=True)
    return o * lax.rsqrt(ms + RMS_EPS) * g


def _silu(x):
    return x / (1.0 + jnp.exp(-x))


def _row_to_col(row):
    n = row.shape[1]
    eye = lax.broadcasted_iota(jnp.int32, (n, n), 0) == lax.broadcasted_iota(jnp.int32, (n, n), 1)
    return jnp.sum(jnp.where(eye, row, 0.0), axis=1, keepdims=True)


def _ada_kernel(c_ref, w_ref, b_ref, o_ref):
    o_ref[...] = _mm(_bf(c_ref[...]), _bf(w_ref[...])) + b_ref[...]


def _ada(c_all, w3, b3, tn=1024):
    n_l, _, n = w3.shape
    m = c_all.shape[0]
    return pl.pallas_call(
        _ada_kernel,
        grid=(n_l, n // tn),
        in_specs=[pl.BlockSpec((m, D), lambda l, j: (0, 0)),
                  pl.BlockSpec((None, D, tn), lambda l, j: (l, 0, j)),
                  pl.BlockSpec((None, 1, tn), lambda l, j: (l, 0, j))],
        out_specs=pl.BlockSpec((None, m, tn), lambda l, j: (l, 0, j)),
        out_shape=jax.ShapeDtypeStruct((n_l, m, n), f32),
        compiler_params=_params("parallel", "parallel"),
        name="ada",
    )(c_all, w3, b3)


def _mod_spec(mod, tiles_per_block):
    return pl.BlockSpec((None, mod.shape[1], D), lambda i, *_: (i // tiles_per_block, 0, 0))


def _row_spec(tm, width):
    return pl.BlockSpec((tm, width), lambda i, *_: (i, 0))


def _const_spec(shape):
    nd = len(shape)
    return pl.BlockSpec(shape, lambda i, *_: (0,) * nd)


def _proj_a_kernel(x_ref, sh_ref, sc_ref, w_ref, wgd_ref, wgu_ref, bg_ref,
                   q_ref, k_ref, v_ref, r_ref, g_ref):
    xm = _bf(x_ref[...] * (1.0 + sc_ref[...]) + sh_ref[...])
    q_ref[...] = _mm(xm, w_ref[:, 0:QK_A]) * (DK_A ** -0.5)
    k_ref[...] = _mm(xm, w_ref[:, QK_A:2 * QK_A])
    v_ref[...] = _mm(xm, w_ref[:, 2 * QK_A:2 * QK_A + VW_A]).astype(v_ref.dtype)
    r_ref[...] = _mm(xm, w_ref[:, 2 * QK_A + VW_A:2 * QK_A + 2 * VW_A])
    gd = _mm(xm, wgd_ref[...])
    z = _mm(_bf(gd), wgu_ref[...]) + bg_ref[...]
    g_ref[...] = (jnp.minimum(z, 0.0) - jnp.log1p(jnp.exp(-jnp.abs(z)))) * (1.0 / GATE_TAU)


def _proj_a(x, sh, sc, tiles_per_block, tm, w_main, w_gd, w_gu, b_g, v_dtype):
    n = x.shape[0]
    outs = [jax.ShapeDtypeStruct((n, QK_A), f32), jax.ShapeDtypeStruct((n, QK_A), f32),
            jax.ShapeDtypeStruct((n, VW_A), v_dtype), jax.ShapeDtypeStruct((n, VW_A), f32),
            jax.ShapeDtypeStruct((n, QK_A), f32)]
    return pl.pallas_call(
        _proj_a_kernel,
        grid=(n // tm,),
        in_specs=[_row_spec(tm, D), _mod_spec(sh, tiles_per_block), _mod_spec(sc, tiles_per_block),
                  _const_spec(w_main.shape), _const_spec(w_gd.shape), _const_spec(w_gu.shape),
                  _const_spec(b_g.shape)],
        out_specs=[_row_spec(tm, QK_A), _row_spec(tm, QK_A), _row_spec(tm, VW_A),
                   _row_spec(tm, VW_A), _row_spec(tm, QK_A)],
        out_shape=outs,
        compiler_params=_params("parallel"),
        name="proj_a",
    )(x, sh, sc, w_main, w_gd, w_gu, b_g)


def _gla_chunk_kernel(q_ref, k_ref, v_ref, r_ref, g_ref, gn_ref, o_ref, s_ref, st_scr, *, n_chunks):
    c = pl.program_id(1)
    C = q_ref.shape[1]

    @pl.when(c == 0)
    def _():
        st_scr[...] = jnp.zeros_like(st_scr)

    row = lax.broadcasted_iota(jnp.int32, (C, C), 0)
    col = lax.broadcasted_iota(jnp.int32, (C, C), 1)
    tril = col <= row
    tril_b = jnp.where(tril, 1.0, 0.0).astype(bf16)
    mid = C // 2 - 1
    for b in range(q_ref.shape[0]):
        g = g_ref[b]
        g1 = _bf(g)
        rem = g - g1.astype(f32)
        g2 = _bf(rem)
        g3 = _bf(rem - g2.astype(f32))
        cum = _mm(tril_b, g1) + _mm(tril_b, g2) + _mm(tril_b, g3)
        for h in range(H_A):
            ks = slice(h * DK_A, (h + 1) * DK_A)
            vs = slice(h * DV_A, (h + 1) * DV_A)
            cum_h = cum[:, ks]
            m = cum_h[mid:mid + 1, :]
            last = cum_h[C - 1:C, :]
            qh = q_ref[b, :, ks]
            kh = k_ref[b, :, ks]
            vb = v_ref[b, :, vs]
            q_in = _bf(qh * jnp.exp(cum_h))
            q_mid = _bf(qh * jnp.exp(cum_h - m))
            k_mid = _bf(kh * jnp.exp(m - cum_h))
            k_last = _bf(kh * jnp.exp(last - cum_h))
            st = st_scr[b, h]
            inter = _mm_nt(q_in, _bf(st))
            attn = jnp.where(tril, _mm_nt(q_mid, k_mid), 0.0)
            o = inter + _mm(_bf(attn), vb)
            st_new = st * jnp.exp(last) + _mm_tn(vb, k_last)
            st_scr[b, h] = st_new
            rh = r_ref[b, :, vs]
            o_ref[b, :, vs] = _bf(_rms_norm(o, gn_ref[...]) * _silu(rh))

    @pl.when(c == n_chunks - 1)
    def _():
        for b in range(q_ref.shape[0]):
            for h in range(H_A):
                s_ref[b, h] = st_scr[b, h].T


def _gla_chunk(q, k, v, r, g, gn, n_seq, t):
    n_chunks = t // GLA_C
    nb = GLA_SEQS
    rows = lambda w: pl.BlockSpec((nb, GLA_C, w), lambda b, c: (b, c, 0))
    seq = lambda a: a.reshape(n_seq, t, a.shape[1])
    o, s_fin = pl.pallas_call(
        functools.partial(_gla_chunk_kernel, n_chunks=n_chunks),
        grid=(n_seq // nb, n_chunks),
        in_specs=[rows(QK_A), rows(QK_A), rows(VW_A), rows(VW_A), rows(QK_A),
                  pl.BlockSpec((1, DV_A), lambda b, c: (0, 0))],
        out_specs=[rows(VW_A), pl.BlockSpec((nb, H_A, DK_A, DV_A), lambda b, c: (b, 0, 0, 0))],
        out_shape=[jax.ShapeDtypeStruct((n_seq, t, VW_A), bf16),
                   jax.ShapeDtypeStruct((n_seq, H_A, DK_A, DV_A), f32)],
        scratch_shapes=[pltpu.VMEM((nb, H_A, DV_A, DK_A), f32)],
        compiler_params=_params("parallel", "arbitrary"),
        name="gla_chunk",
    )(seq(q), seq(k), seq(v), seq(r), seq(g), gn)
    return o.reshape(n_seq * t, VW_A), s_fin


def _gla_step_kernel(q_ref, k_ref, v_ref, r_ref, g_ref, gn_ref, s_ref, o_ref, sn_ref):
    for h in range(H_A):
        ks = slice(h * DK_A, (h + 1) * DK_A)
        vs = slice(h * DV_A, (h + 1) * DV_A)
        q_col = _row_to_col(q_ref[:, ks])
        k_col = _row_to_col(k_ref[:, ks])
        eg_col = _row_to_col(jnp.exp(g_ref[:, ks]))
        s_new = eg_col * s_ref[h] + k_col * v_ref[:, vs]
        sn_ref[h] = s_new
        o = jnp.sum(q_col * s_new, axis=0, keepdims=True)
        rh = r_ref[:, vs]
        o_ref[:, vs] = _rms_norm(o, gn_ref[...]) * _silu(rh)


def _gla_step(q, k, v, r, g, gn, state):
    n = q.shape[0]
    vec = lambda w: pl.BlockSpec((None, 1, w), lambda b: (b, 0, 0))
    st = pl.BlockSpec((None, H_A, DK_A, DV_A), lambda b: (b, 0, 0, 0))
    o, s_new = pl.pallas_call(
        _gla_step_kernel,
        grid=(n,),
        in_specs=[vec(QK_A), vec(QK_A), vec(VW_A), vec(VW_A), vec(QK_A),
                  pl.BlockSpec((1, DV_A), lambda b: (0, 0)), st],
        out_specs=[vec(VW_A), st],
        out_shape=[jax.ShapeDtypeStruct((n, 1, VW_A), f32),
                   jax.ShapeDtypeStruct((n, H_A, DK_A, DV_A), f32)],
        compiler_params=_params("parallel"),
        name="gla_step",
    )(q.reshape(n, 1, QK_A), k.reshape(n, 1, QK_A), v.reshape(n, 1, VW_A), r.reshape(n, 1, VW_A),
      g.reshape(n, 1, QK_A), gn, state)
    return o.reshape(n, VW_A), s_new


def _route(probs):
    p = [probs[e:e + 1, :] for e in range(N_EXPERTS)]
    scores = []
    for gi in range(N_GROUPS):
        a0, a1, a2, a3 = p[4 * gi:4 * gi + 4]
        hi1, lo1 = jnp.maximum(a0, a1), jnp.minimum(a0, a1)
        hi2, lo2 = jnp.maximum(a2, a3), jnp.minimum(a2, a3)
        scores.append(jnp.maximum(hi1, hi2) + jnp.maximum(jnp.minimum(hi1, hi2), jnp.maximum(lo1, lo2)))
    best = scores[0]
    gidx = jnp.zeros(best.shape, jnp.int32)
    for gi in range(1, N_GROUPS):
        upd = scores[gi] > best
        best = jnp.where(upd, scores[gi], best)
        gidx = jnp.where(upd, gi, gidx)
    a = [jnp.where(gidx == 0, p[j], jnp.where(gidx == 1, p[4 + j], jnp.where(gidx == 2, p[8 + j], p[12 + j])))
         for j in range(GROUP_SIZE)]
    v1 = a[0]
    i1 = jnp.zeros(best.shape, jnp.int32)
    for j in range(1, GROUP_SIZE):
        upd = a[j] > v1
        v1 = jnp.where(upd, a[j], v1)
        i1 = jnp.where(upd, j, i1)
    v2 = jnp.full(best.shape, -1.0, f32)
    i2 = jnp.zeros(best.shape, jnp.int32)
    for j in range(GROUP_SIZE):
        upd = jnp.logical_and(i1 != j, a[j] > v2)
        v2 = jnp.where(upd, a[j], v2)
        i2 = jnp.where(upd, j, i2)
    tot = v1 + v2
    w1 = v1 / tot
    w2 = v2 / tot
    first_low = i1 < i2
    lo = jnp.minimum(i1, i2)
    hi = jnp.maximum(i1, i2)
    pair = lo * 3 - ((lo * (lo - 1)) >> 1) + hi - lo - 1
    cls = gidx * 6 + pair
    return cls, jnp.where(first_low, w1, w2), jnp.where(first_low, w2, w1)


def _post_kernel(a_ref, w_ref, x_ref, g1_ref, lng_ref, lnb_ref, sh2_ref, sc2_ref, wr_ref, br_ref,
                 x1_ref, rows_ref, wts_ref, meta_ref, cnt_ref, cnt_scr):
    i = pl.program_id(0)
    tm = x_ref.shape[0]

    @pl.when(i == 0)
    def _():
        cnt_scr[...] = jnp.zeros_like(cnt_scr)

    mix = _mm(_bf(a_ref[...]), w_ref[...])
    x1 = _layer_norm(ALPHA * x_ref[...] + g1_ref[...] * mix, lng_ref[...], lnb_ref[...])
    x1_ref[...] = x1
    xm = x1 * (1.0 + sc2_ref[...]) + sh2_ref[...]
    _store_tokens(rows_ref, xm, 0, SUB)

    x_hi = _bf(xm)
    x_lo = _bf(xm - x_hi.astype(f32))
    wr = wr_ref[...]
    l_hi = _mm_nt(wr, x_hi)
    l_lo = _mm_nt(wr[0:N_EXPERTS], x_lo)
    logits = l_hi[0:N_EXPERTS] + l_hi[N_EXPERTS:2 * N_EXPERTS] + l_lo + br_ref[...]
    mx = jnp.max(logits, axis=0, keepdims=True)
    ex = jnp.exp(logits - mx)
    probs = ex / jnp.sum(ex, axis=0, keepdims=True)
    cls, w_lo, w_hi = _route(probs)

    crow = lax.broadcasted_iota(jnp.int32, (32, tm), 0)
    onehot = crow == cls
    oh = jnp.where(onehot, 1.0, 0.0)
    before = lax.broadcasted_iota(jnp.int32, (tm, tm), 0) < lax.broadcasted_iota(jnp.int32, (tm, tm), 1)
    prefix = _mm(_bf(oh), jnp.where(before, 1.0, 0.0).astype(bf16))
    carry = cnt_scr[...]
    rank = jnp.sum(jnp.where(onehot, prefix + carry, 0.0), axis=0, keepdims=True)
    cnt_new = carry + jnp.sum(oh, axis=1, keepdims=True)
    cnt_scr[...] = cnt_new
    cnt_ref[...] = jnp.broadcast_to(cnt_new, cnt_ref.shape)

    mrow = lax.broadcasted_iota(jnp.int32, (8, tm), 0)
    meta_ref[...] = jnp.where(mrow == 0, cls, jnp.where(mrow == 1, rank.astype(jnp.int32), 0))

    lane = lax.broadcasted_iota(jnp.int32, (tm, LANES), 1)
    wts_ref[...] = jnp.where(lane == 0, _row_to_col(w_lo), jnp.where(lane == 1, _row_to_col(w_hi), 0.0))


def _post(a, w_o, x, g1, lng, lnb, sh2, sc2, tiles_per_block, tm, wr_t, br_col):
    n = x.shape[0]
    mod = lambda arr: _mod_spec(arr, tiles_per_block)
    return pl.pallas_call(
        _post_kernel,
        grid=(n // tm,),
        in_specs=[_row_spec(tm, D), _const_spec(w_o.shape), _row_spec(tm, D), mod(g1),
                  _const_spec(lng.shape), _const_spec(lnb.shape), mod(sh2), mod(sc2),
                  _const_spec(wr_t.shape), _const_spec(br_col.shape)],
        out_specs=[_row_spec(tm, D), _row_spec(tm * SUB, LANES), _row_spec(tm, LANES),
                   pl.BlockSpec((8, tm), lambda i: (0, i)),
                   pl.BlockSpec((32, LANES), lambda i: (0, 0))],
        out_shape=[jax.ShapeDtypeStruct((n, D), f32), jax.ShapeDtypeStruct((n * SUB, LANES), f32),
                   jax.ShapeDtypeStruct((n, LANES), f32), jax.ShapeDtypeStruct((8, n), jnp.int32),
                   jax.ShapeDtypeStruct((32, LANES), f32)],
        scratch_shapes=[pltpu.VMEM((32, 1), f32)],
        compiler_params=_params("arbitrary"),
        name="post",
    )(a, w_o, x, g1, lng, lnb, sh2, sc2, wr_t, br_col)


def _scatter_kernel(pos_ref, src_ref, dst_in_ref, dst_ref, sem, *, tm, rows):
    del dst_in_ref
    base = pl.program_id(0) * tm

    def token_copy(r):
        slot = pl.multiple_of(pos_ref[base + r] * rows, rows)
        return pltpu.make_async_copy(src_ref.at[pl.ds(pl.multiple_of(r * rows, rows), rows)],
                                     dst_ref.at[pl.ds(slot, rows)], sem)

    def start(r, carry):
        token_copy(r).start()
        return carry

    def wait(r, carry):
        token_copy(r).wait()
        return carry

    lax.fori_loop(0, tm, start, 0, unroll=8)
    lax.fori_loop(0, tm, wait, 0, unroll=8)


def _scatter_rows(pos, src, dst, rows):
    n = pos.shape[0]
    tm = min(TM_PERM, n)
    return pl.pallas_call(
        functools.partial(_scatter_kernel, tm=tm, rows=rows),
        grid_spec=pltpu.PrefetchScalarGridSpec(
            num_scalar_prefetch=1, grid=(n // tm,),
            in_specs=[pl.BlockSpec((tm * rows, LANES), lambda t, pos: (t, 0)),
                      pl.BlockSpec(memory_space=pl.ANY)],
            out_specs=pl.BlockSpec(memory_space=pl.ANY),
            scratch_shapes=[pltpu.SemaphoreType.DMA(())]),
        out_shape=jax.ShapeDtypeStruct(dst.shape, dst.dtype),
        input_output_aliases={2: 0},
        compiler_params=_params("arbitrary"),
        name="scatter_rows",
    )(pos, src, dst)


def _gather_tokens(pos_ref, src_ref, buf_ref, sem, tile, tm, rows, wait):
    base = tile * tm

    def token_copy(r):
        slot = pl.multiple_of(pos_ref[base + r] * rows, rows)
        return pltpu.make_async_copy(src_ref.at[pl.ds(slot, rows)],
                                     buf_ref.at[pl.ds(pl.multiple_of(r * rows, rows), rows)], sem)

    def step(r, carry):
        if wait:
            token_copy(r).wait()
        else:
            token_copy(r).start()
        return carry

    lax.fori_loop(0, tm, step, 0, unroll=8)


def _prefetched_tokens(pos_ref, src_ref, buf_ref, sem_ref, tm, rows):
    i = pl.program_id(0)
    slot = i % 2

    @pl.when(i == 0)
    def _():
        _gather_tokens(pos_ref, src_ref, buf_ref.at[0], sem_ref.at[0], 0, tm, rows, wait=False)

    @pl.when(i + 1 < pl.num_programs(0))
    def _():
        _gather_tokens(pos_ref, src_ref, buf_ref.at[1 - slot], sem_ref.at[1 - slot], i + 1, tm, rows, wait=False)

    _gather_tokens(pos_ref, src_ref, buf_ref.at[slot], sem_ref.at[slot], i, tm, rows, wait=True)
    return buf_ref.at[slot]


def _experts_kernel(elo_ref, ehi_ref, valid_ref, rows_ref,
                    wg_lo, wu_lo, wd_lo, wg_hi, wu_hi, wd_hi, o_ref):
    t = pl.program_id(0)

    @pl.when(valid_ref[t] == 1)
    def _():
        xb = _bf(_load_tokens(rows_ref, TM_E, SUB))

        def ffn(wg, wu, wd):
            hid = _silu(_mm(xb, wg[...])) * _mm(xb, wu[...])
            return _mm(_bf(hid), wd[...])

        _store_tokens(o_ref, ffn(wg_lo, wu_lo, wd_lo), 0, 2 * SUB)
        _store_tokens(o_ref, ffn(wg_hi, wu_hi, wd_hi), SUB, 2 * SUB)

    @pl.when(valid_ref[t] == 0)
    def _():
        o_ref[...] = jnp.zeros_like(o_ref)


def _experts(e_lo, e_hi, valid, rows, wg, wu, wd, layer):
    n_tiles = rows.shape[0] // (TM_E * SUB)
    up = lambda sel: pl.BlockSpec((None, None, D, D_FF), lambda t, lo, hi, v: (layer, sel(lo, hi)[t], 0, 0))
    down = lambda sel: pl.BlockSpec((None, None, D_FF, D), lambda t, lo, hi, v: (layer, sel(lo, hi)[t], 0, 0))
    first = lambda lo, hi: lo
    second = lambda lo, hi: hi
    return pl.pallas_call(
        _experts_kernel,
        grid_spec=pltpu.PrefetchScalarGridSpec(
            num_scalar_prefetch=3, grid=(n_tiles,),
            in_specs=[pl.BlockSpec((TM_E * SUB, LANES), lambda t, lo, hi, v: (t, 0)),
                      up(first), up(first), down(first), up(second), up(second), down(second)],
            out_specs=pl.BlockSpec((TM_E * 2 * SUB, LANES), lambda t, lo, hi, v: (t, 0))),
        out_shape=jax.ShapeDtypeStruct((2 * rows.shape[0], LANES), f32),
        compiler_params=_params("parallel"),
        name="experts",
    )(e_lo, e_hi, valid, rows, wg, wu, wd, wg, wu, wd)


def _moe(rows_p, meta_p, cnt_p, rows_s, meta_s, cnt_s, wg, wu, wd, layer):
    n_p, n_s = meta_p.shape[1], meta_s.shape[1]
    n_tiles = -(-(n_p + n_s) // TM_E) + N_CLASSES
    cp = cnt_p[:N_CLASSES, 0].astype(jnp.int32)
    cs = cnt_s[:N_CLASSES, 0].astype(jnp.int32)
    tiles_c = (cp + cs + TM_E - 1) // TM_E
    tile_end = jnp.cumsum(tiles_c)
    row_off = (tile_end - tiles_c) * TM_E
    pos_p = row_off[meta_p[0]] + meta_p[1]
    pos_s = row_off[meta_s[0]] + cp[meta_s[0]] + meta_s[1]
    t_idx = jnp.arange(n_tiles, dtype=jnp.int32)
    valid = (t_idx < tile_end[-1]).astype(jnp.int32)
    tile_cls = jnp.minimum(jnp.sum((tile_end[None, :] <= t_idx[:, None]).astype(jnp.int32), axis=1), N_CLASSES - 1)
    pairs = [(a, b) for a in range(GROUP_SIZE) for b in range(a + 1, GROUP_SIZE)]
    lo_tab = jnp.asarray([gi * GROUP_SIZE + a for gi in range(N_GROUPS) for a, _ in pairs], jnp.int32)
    hi_tab = jnp.asarray([gi * GROUP_SIZE + b for gi in range(N_GROUPS) for _, b in pairs], jnp.int32)
    e_lo, e_hi = lo_tab[tile_cls], hi_tab[tile_cls]

    buf = jnp.zeros((n_tiles * TM_E * SUB, LANES), f32)
    buf = _scatter_rows(pos_p, rows_p, buf, SUB)
    buf = _scatter_rows(pos_s, rows_s, buf, SUB)
    ff_sorted = _experts(e_lo, e_hi, valid, buf, wg, wu, wd, layer)
    return ff_sorted, pos_p, pos_s


def _ffn_out(pos_ref, sorted_ref, wts_ref, buf_ref, sem_ref, n):
    pair_ref = _prefetched_tokens(pos_ref, sorted_ref, buf_ref, sem_ref, n, 2 * SUB)
    wts = wts_ref[...]
    lo = jnp.concatenate([pair_ref[pl.ds(h, n, stride=2 * SUB), :] for h in range(SUB)], axis=1)
    hi = jnp.concatenate([pair_ref[pl.ds(SUB + h, n, stride=2 * SUB), :] for h in range(SUB)], axis=1)
    return wts[:, 0:1] * lo + wts[:, 1:2] * hi


def _mid_kernel(pos_ref, x1_ref, sorted_ref, wts_ref, g2_ref, lng_ref, lnb_ref, shkv_ref, sckv_ref, shq_ref,
                scq_ref, wkv_ref, wq_ref, x2_ref, k_ref, v_ref, kb_ref, vb_ref, q_ref, buf_ref, sem_ref):
    ff = _ffn_out(pos_ref, sorted_ref, wts_ref, buf_ref, sem_ref, x1_ref.shape[0])
    x2 = _layer_norm(ALPHA * x1_ref[...] + g2_ref[...] * ff, lng_ref[...], lnb_ref[...])
    x2_ref[...] = x2
    xkv = _bf(x2 * (1.0 + sckv_ref[...]) + shkv_ref[...])
    kk = _mm(xkv, wkv_ref[:, 0:D])
    vv = _mm(xkv, wkv_ref[:, D:2 * D])
    _store_tokens(k_ref, kk, 0, SUB)
    _store_tokens(v_ref, vv, 0, SUB)
    kb_ref[...] = _bf(kk)
    vb_ref[...] = _bf(vv)
    xq = _bf(x2 * (1.0 + scq_ref[...]) + shq_ref[...])
    q_ref[...] = _bf(_mm(xq, wq_ref[...]) * (DH_B ** -0.5))


def _pair_scratch(tm):
    return [pltpu.VMEM((2, tm * 2 * SUB, LANES), f32), pltpu.SemaphoreType.DMA((2,))]


def _mid(pos, x1, ff_sorted, wts, g2, lng, lnb, shkv, sckv, shq, scq, tiles_per_block, tm, w_kv, w_q):
    n = x1.shape[0]
    mod = lambda arr: _mod_spec(arr, tiles_per_block)
    sds = lambda dt: jax.ShapeDtypeStruct((n, D), dt)
    kv_sds = jax.ShapeDtypeStruct((n * SUB, LANES), f32)
    return pl.pallas_call(
        _mid_kernel,
        grid_spec=pltpu.PrefetchScalarGridSpec(
            num_scalar_prefetch=1, grid=(n // tm,),
            in_specs=[_row_spec(tm, D), pl.BlockSpec(memory_space=pl.ANY), _row_spec(tm, LANES), mod(g2),
                      _const_spec(lng.shape), _const_spec(lnb.shape),
                      mod(shkv), mod(sckv), mod(shq), mod(scq), _const_spec(w_kv.shape), _const_spec(w_q.shape)],
            out_specs=[_row_spec(tm, D), _row_spec(tm * SUB, LANES), _row_spec(tm * SUB, LANES),
                       _row_spec(tm, D), _row_spec(tm, D), _row_spec(tm, D)],
            scratch_shapes=_pair_scratch(tm)),
        out_shape=[sds(f32), kv_sds, kv_sds, sds(bf16), sds(bf16), sds(bf16)],
        compiler_params=_params("arbitrary"),
        name="mid",
    )(pos, x1, ff_sorted, wts, g2, lng, lnb, shkv, sckv, shq, scq, w_kv, w_q)


def _lambda(lq1_ref, lk1_ref, lq2_ref, lk2_ref, lam_init):
    s1 = jnp.sum(lq1_ref[...] * lk1_ref[...], axis=1, keepdims=True)
    s2 = jnp.sum(lq2_ref[...] * lk2_ref[...], axis=1, keepdims=True)
    return jnp.exp(s1) - jnp.exp(s2) + lam_init


def _online_update(s, v, m_ref, l_ref, acc_ref):
    m_old = m_ref[...]
    m_new = jnp.maximum(m_old, jnp.max(s, axis=1, keepdims=True))
    p = jnp.exp(s - jnp.tile(m_new, (1, s.shape[1] // LANES)))
    corr = jnp.exp(m_old - m_new)
    l_ref[...] = l_ref[...] * corr + jnp.sum(p, axis=1, keepdims=True)
    acc_ref[...] = acc_ref[...] * jnp.tile(corr, (1, acc_ref.shape[1] // LANES)) + _mm(p.astype(v.dtype), v)
    m_ref[...] = m_new


def _dattn_kernel(slope_ref, q_ref, k_ref, v_ref, lq1_ref, lk1_ref, lq2_ref, lk2_ref, subln_ref,
                  o_ref, m_scr, l_scr, acc_scr, *, lam_init):
    t = q_ref.shape[0]
    slope = slope_ref[pl.program_id(1)]
    lam = _lambda(lq1_ref, lk1_ref, lq2_ref, lk2_ref, lam_init)
    first_map = lax.broadcasted_iota(jnp.int32, (1, 2 * DH_B), 1) < DH_B
    row = lax.broadcasted_iota(jnp.int32, (2 * TQ, 1), 0)
    rel_q = (row & (TH - 1)) + ((row >> 9) << 8)
    assert TH == 256 and TQ == 512

    def attend(qs, rel_rows, rows, q0, k0, nk, masked):
        s = _mm_nt(qs, k_ref[pl.ds(k0, nk), :])
        rel_k = lax.broadcasted_iota(jnp.int32, (1, nk), 1) + (k0 - q0)
        s = s + slope * rel_k.astype(f32)
        if masked:
            s = jnp.where(rel_k <= rel_rows, s, NEG_BIG)
        _online_update(s, v_ref[pl.ds(k0, nk), :], m_scr.at[rows], l_scr.at[rows], acc_scr.at[rows])

    for i in range(t // TQ):
        q0 = i * TQ
        m_scr[...] = jnp.full(m_scr.shape, NEG_BIG, f32)
        l_scr[...] = jnp.zeros_like(l_scr)
        acc_scr[...] = jnp.zeros_like(acc_scr)
        blocks = []
        for half in range(2):
            q = q_ref[q0 + half * TH:q0 + (half + 1) * TH, :]
            zero = jnp.zeros_like(q)
            blocks += [jnp.where(first_map, q, zero), jnp.where(first_map, zero, q)]
        qs = jnp.concatenate(blocks, axis=0)
        every = pl.ds(0, 2 * TQ)
        second_half = pl.ds(TQ, TQ)

        def below_diagonal(j, carry):
            attend(qs, rel_q, every, q0, pl.multiple_of(j * TQ, TQ), TQ, False)
            return carry

        lax.fori_loop(0, i, below_diagonal, 0)
        attend(qs, rel_q, every, q0, q0, TH, True)
        attend(qs[TQ:2 * TQ], rel_q[TQ:2 * TQ], second_half, q0, q0 + TH, TH, True)

        for half in range(2):
            r0, r1 = 2 * half * TH, (2 * half + 1) * TH
            o1 = acc_scr[r0:r0 + TH] / l_scr[r0:r0 + TH]
            o2 = acc_scr[r1:r1 + TH] / l_scr[r1:r1 + TH]
            o = o1 - lam * o2
            o_ref[q0 + half * TH:q0 + (half + 1) * TH, :] = _bf(_rms_norm(o, subln_ref[...]) * (1.0 - lam_init))


def _dattn(slopes, q, k, v, lq1, lk1, lq2, lk2, subln, n_seq, t, lam_init):
    head = pl.BlockSpec((t, 2 * DH_B), lambda b, h: (b, h))
    small = lambda w: pl.BlockSpec((1, w), lambda b, h: (0, 0))
    return pl.pallas_call(
        functools.partial(_dattn_kernel, lam_init=lam_init),
        grid=(n_seq, H_B),
        in_specs=[pl.BlockSpec(memory_space=pltpu.SMEM), head, head, head,
                  small(DH_B), small(DH_B), small(DH_B), small(DH_B), small(DV_B)],
        out_specs=head,
        out_shape=jax.ShapeDtypeStruct((n_seq * t, H_B * DV_B), bf16),
        scratch_shapes=[pltpu.VMEM((2 * TQ, LANES), f32), pltpu.VMEM((2 * TQ, LANES), f32),
                        pltpu.VMEM((2 * TQ, DV_B), f32)],
        compiler_params=_params("parallel", "parallel"),
        name="dattn_prompt",
    )(slopes, q, k, v, lq1, lk1, lq2, lk2, subln)


def _dattn_step_kernel(pt_ref, q_ref, kn_ref, vn_ref, nslope_ref, lq1_ref, lk1_ref, lq2_ref, lk2_ref,
                       subln_ref, *rest, past_len, lam_init):
    n_pg = PAGES_PER_STEP
    k_refs, v_refs = rest[0:n_pg], rest[n_pg:2 * n_pg]
    o_ref, qbd_scr, m_scr, l_scr, acc_scr = rest[2 * n_pg:]
    j = pl.program_id(1)
    rows = 2 * H_B

    @pl.when(j == 0)
    def _():
        lane = lax.broadcasted_iota(jnp.int32, (rows, D), 1)
        rowi = lax.broadcasted_iota(jnp.int32, (rows, D), 0)
        q = jnp.broadcast_to(q_ref[...].astype(f32), (rows, D))
        qbd_scr[...] = jnp.where((lane >> 6) == rowi, q, 0.0)
        m_scr[...] = jnp.full(m_scr.shape, NEG_BIG, f32)
        l_scr[...] = jnp.zeros_like(l_scr)
        acc_scr[...] = jnp.zeros_like(acc_scr)

    qbd = qbd_scr[...]
    s = jnp.concatenate([_mm_nt(qbd, _load_tokens(k_refs[u], PAGE, SUB)) for u in range(n_pg)], axis=1)
    kpos = j * (n_pg * PAGE) + lax.broadcasted_iota(jnp.int32, (1, n_pg * PAGE), 1)
    s = s + nslope_ref[...] * (past_len - kpos).astype(f32)
    m_old = m_scr[...]
    m_new = jnp.maximum(m_old, jnp.max(s, axis=1, keepdims=True))
    p = jnp.exp(s - jnp.tile(m_new, (1, n_pg)))
    corr = jnp.exp(m_old - m_new)
    l_scr[...] = l_scr[...] * corr + jnp.sum(p, axis=1, keepdims=True)
    pv = _mm(p[:, 0:PAGE], _load_tokens(v_refs[0], PAGE, SUB))
    for u in range(1, n_pg):
        pv = pv + _mm(p[:, u * PAGE:(u + 1) * PAGE], _load_tokens(v_refs[u], PAGE, SUB))
    acc_scr[...] = acc_scr[...] * jnp.tile(corr, (1, SUB)) + pv
    m_scr[...] = m_new

    @pl.when(j == pl.num_programs(1) - 1)
    def _():
        kn = jnp.concatenate([kn_ref[h:h + 1, :] for h in range(H_B)], axis=1)
        vn = jnp.concatenate([vn_ref[h:h + 1, :] for h in range(H_B)], axis=1)
        kn_b = _bf(kn).astype(f32)
        vn_b = _bf(vn).astype(f32)
        s = jnp.sum(qbd * kn_b, axis=1, keepdims=True)
        m_old = m_scr[...]
        m_new = jnp.maximum(m_old, s)
        p = jnp.exp(s - m_new)
        corr = jnp.exp(m_old - m_new)
        l_fin = l_scr[...] * corr + p
        acc = acc_scr[...] * jnp.tile(corr, (1, SUB)) + _bf(jnp.tile(p, (1, SUB))).astype(f32) * vn_b
        lane = lax.broadcasted_iota(jnp.int32, (rows, D), 1)
        rowi = lax.broadcasted_iota(jnp.int32, (rows, D), 0)
        a = jnp.where((lane >> 7) == (rowi >> 1), acc / jnp.tile(l_fin, (1, SUB)), 0.0)
        a1 = jnp.sum(jnp.where((rowi & 1) == 0, a, 0.0), axis=0, keepdims=True)
        a2 = jnp.sum(jnp.where((rowi & 1) == 1, a, 0.0), axis=0, keepdims=True)
        o = a1 - _lambda(lq1_ref, lk1_ref, lq2_ref, lk2_ref, lam_init) * a2
        for h in range(H_B):
            hs = slice(h * DV_B, (h + 1) * DV_B)
            o_ref[:, hs] = _rms_norm(o[:, hs], subln_ref[...]) * (1.0 - lam_init)


def _dattn_step(page_table, q, k_new, v_new, cache_k, cache_v, nslope_rows, lq1, lk1, lq2, lk2, subln,
                past_len, lam_init):
    n, n_pages = page_table.shape
    n_pool = cache_k.shape[0]
    ck = cache_k.reshape(n_pool, PAGE * H_B, DV_B)
    cv = cache_v.reshape(n_pool, PAGE * H_B, DV_B)
    n_pg = PAGES_PER_STEP
    vec = pl.BlockSpec((None, 1, D), lambda b, j, pt: (b, 0, 0))
    new_kv = pl.BlockSpec((SUB, LANES), lambda b, j, pt: (b, 0))
    const = lambda shape: pl.BlockSpec(shape, lambda b, j, pt: (0,) * len(shape))

    def page(u):
        return pl.BlockSpec((None, PAGE * H_B, DV_B), lambda b, j, pt: (pt[b, j * n_pg + u], 0, 0))

    pages = [page(u) for u in range(n_pg)]
    out = pl.pallas_call(
        functools.partial(_dattn_step_kernel, past_len=past_len, lam_init=lam_init),
        grid_spec=pltpu.PrefetchScalarGridSpec(
            num_scalar_prefetch=1, grid=(n, n_pages // n_pg),
            in_specs=[vec, new_kv, new_kv, const((2 * H_B, 1)), const((1, DH_B)), const((1, DH_B)),
                      const((1, DH_B)), const((1, DH_B)), const((1, DV_B))] + pages + pages,
            out_specs=vec,
            scratch_shapes=[pltpu.VMEM((2 * H_B, D), f32), pltpu.VMEM((2 * H_B, LANES), f32),
                            pltpu.VMEM((2 * H_B, LANES), f32), pltpu.VMEM((2 * H_B, D), f32)]),
        out_shape=jax.ShapeDtypeStruct((n, 1, D), f32),
        compiler_params=_params("parallel", "arbitrary"),
        name="dattn_step",
    )(page_table, q.reshape(n, 1, D), k_new, v_new, nslope_rows,
      lq1, lk1, lq2, lk2, subln, *([ck] * n_pg), *([cv] * n_pg))
    return out.reshape(n, D)


def _final_kernel(pos_ref, x_ref, sorted_ref, wts_ref, g2_ref, lng_ref, lnb_ref, y_ref, buf_ref, sem_ref):
    ff = _ffn_out(pos_ref, sorted_ref, wts_ref, buf_ref, sem_ref, x_ref.shape[0])
    y_ref[...] = _layer_norm(ALPHA * x_ref[...] + g2_ref[...] * ff, lng_ref[...], lnb_ref[...])


def _final(pos, x, ff_sorted, wts, g2, lng, lnb, tiles_per_block, tm):
    n = x.shape[0]
    return pl.pallas_call(
        _final_kernel,
        grid_spec=pltpu.PrefetchScalarGridSpec(
            num_scalar_prefetch=1, grid=(n // tm,),
            in_specs=[_row_spec(tm, D), pl.BlockSpec(memory_space=pl.ANY), _row_spec(tm, LANES),
                      _mod_spec(g2, tiles_per_block), _const_spec(lng.shape), _const_spec(lnb.shape)],
            out_specs=_row_spec(tm, D),
            scratch_shapes=_pair_scratch(tm)),
        out_shape=jax.ShapeDtypeStruct((n, D), f32),
        compiler_params=_params("arbitrary"),
        name="final",
    )(pos, x, ff_sorted, wts, g2, lng, lnb)


def kernel(x_prompt, x_sample, c_prompt, c_sample, state_gla, cache_k, cache_v, page_table, w_ada, b_ada,
           ln_mix_g, ln_mix_b, ln_ffn_g, ln_ffn_b, w_in_a, w_gate_up_a, b_gate_a, gn_a, w_o_a, w_ada_kv,
           b_ada_kv, w_kv, w_q_b, lam_q1, lam_k1, lam_q2, lam_k2, subln_b, w_o_b, w_router, b_router,
           w_exp_gate, w_exp_up, w_exp_down):
    n_seq, t, _ = x_prompt.shape
    n_smp = x_sample.shape[0]
    n_p = n_seq * t
    past_len = page_table.shape[1] * PAGE
    xp = x_prompt.reshape(n_p, D)
    xs = x_sample.reshape(n_smp, D)
    tpb = t // TM

    c_all = jnp.concatenate([c_prompt, c_sample], axis=0)
    ada = _ada(c_all, w_ada, b_ada.reshape(DEPTH, 1, 6 * D))
    ada_kv = _ada(c_all, w_ada_kv.reshape(1, D, 2 * D), b_ada_kv.reshape(1, 1, 2 * D))[0]

    def mods(table, idx):
        part = table[:, idx * D:(idx + 1) * D]
        return part[:n_seq].reshape(n_seq, 1, D), part[n_seq:].reshape(1, n_smp, D)

    row = lambda a: a.reshape(1, -1)

    w_in = w_in_a[0]
    w_main = _bf(w_in[:, :2 * QK_A + 2 * VW_A])
    w_gd = _bf(jnp.pad(w_in[:, 2 * QK_A + 2 * VW_A:], ((0, 0), (0, LANES - GATE_RANK))))
    w_gu = _bf(jnp.pad(w_gate_up_a[0], ((0, LANES - GATE_RANK), (0, 0))))
    wr_t = w_router.T
    wr_hi = _bf(wr_t)
    wr_split = jnp.concatenate([wr_hi, _bf(wr_t - wr_hi.astype(f32))], axis=0)
    br_col = b_router.reshape(N_EXPERTS, 1)
    wg, wu, wd = _bf(w_exp_gate), _bf(w_exp_up), _bf(w_exp_down)

    sh1, sc1, g1, sh2, sc2, g2 = [mods(ada[0], i) for i in range(6)]
    b_g = row(b_gate_a[0])
    qp, kp, vp, rp, gp = _proj_a(xp, sh1[0], sc1[0], tpb, TM, w_main, w_gd, w_gu, b_g, bf16)
    qs, ks, vs, rs, gs = _proj_a(xs, sh1[1], sc1[1], 1, n_smp, w_main, w_gd, w_gu, b_g, f32)
    gn = row(gn_a[0])
    ap, s_prompt = _gla_chunk(qp, kp, vp, rp, gp, gn, n_seq, t)
    as_, s_sample = _gla_step(qs, ks, vs, rs, gs, gn, state_gla[0])

    w_oa = _bf(w_o_a[0])
    lng, lnb = row(ln_mix_g[0]), row(ln_mix_b[0])
    x1p, rows_p, wts_p, meta_p, cnt_p = _post(ap, w_oa, xp, g1[0], lng, lnb, sh2[0], sc2[0], tpb, TM,
                                              wr_split, br_col)
    x1s, rows_s, wts_s, meta_s, cnt_s = _post(as_, w_oa, xs, g1[1], lng, lnb, sh2[1], sc2[1], 1, n_smp,
                                              wr_split, br_col)
    ff_sorted, pos_p, pos_s = _moe(rows_p, meta_p, cnt_p, rows_s, meta_s, cnt_s, wg, wu, wd, 0)

    shkv, sckv = mods(ada_kv, 0), mods(ada_kv, 1)
    sh1, sc1, g1b, sh2b, sc2b, g2b = [mods(ada[1], i) for i in range(6)]
    lng, lnb = row(ln_ffn_g[0]), row(ln_ffn_b[0])
    w_kvb, w_qb = _bf(w_kv), _bf(w_q_b[0])
    x2p, k_p, v_p, kb_p, vb_p, q_p = _mid(pos_p, x1p, ff_sorted, wts_p, g2[0], lng, lnb, shkv[0], sckv[0],
                                          sh1[0], sc1[0], tpb, TM, w_kvb, w_qb)
    x2s, k_s, v_s, _, _, q_s = _mid(pos_s, x1s, ff_sorted, wts_s, g2[1], lng, lnb, shkv[1], sckv[1],
                                    sh1[1], sc1[1], 1, n_smp, w_kvb, w_qb)

    lam_init = 0.8 - 0.6 * math.exp(-0.3 * 1)
    slopes = jnp.asarray([2.0 ** (-8.0 * (i + 1) / H_B) for i in range(H_B)], f32)
    nslope_rows = -jnp.repeat(slopes, 2).reshape(2 * H_B, 1)
    lam_args = (row(lam_q1[0]), row(lam_k1[0]), row(lam_q2[0]), row(lam_k2[0]), row(subln_b[0]))
    bp = _dattn(slopes, q_p, kb_p, vb_p, *lam_args, n_seq, t, lam_init)
    bs = _dattn_step(page_table, q_s, k_s, v_s, cache_k, cache_v, nslope_rows, *lam_args, past_len, lam_init)

    w_ob = _bf(w_o_b[0])
    lng, lnb = row(ln_mix_g[1]), row(ln_mix_b[1])
    x3p, rows_p, wts_p, meta_p, cnt_p = _post(bp, w_ob, x2p, g1b[0], lng, lnb, sh2b[0], sc2b[0], tpb, TM,
                                              wr_split, br_col)
    x3s, rows_s, wts_s, meta_s, cnt_s = _post(bs, w_ob, x2s, g1b[1], lng, lnb, sh2b[1], sc2b[1], 1, n_smp,
                                              wr_split, br_col)
    ff_sorted, pos_p, pos_s = _moe(rows_p, meta_p, cnt_p, rows_s, meta_s, cnt_s, wg, wu, wd, 1)
    lng, lnb = row(ln_ffn_g[1]), row(ln_ffn_b[1])
    y_p = _final(pos_p, x3p, ff_sorted, wts_p, g2b[0], lng, lnb, tpb, TM)
    y_s = _final(pos_s, x3s, ff_sorted, wts_s, g2b[1], lng, lnb, 1, n_smp)

    n_pg_p = t // PAGE
    return (y_p.reshape(n_seq, t, D), y_s.reshape(n_smp, 1, D),
            s_prompt[None], s_sample[None],
            k_p.reshape(n_seq, n_pg_p, PAGE, H_B, 2 * DH_B), v_p.reshape(n_seq, n_pg_p, PAGE, H_B, DV_B),
            k_s.reshape(n_smp, 1, H_B, 2 * DH_B), v_s.reshape(n_smp, 1, H_B, DV_B))
```
